```python
import jax, jax.numpy as jnp
from jax import lax
import numpy as np

D_MODEL = 1024
BATCH = 8
SEQ = 4096
DEPTH = 1

GRID_W = 64
CTX_LEN = 256
DN_HEADS = 8
DN_HEAD_DIM = 128
DN_WIDTH = DN_HEADS * DN_HEAD_DIM
SHORT_CONV = 5
CHUNK = 64
POOL_GROUPS = 4
POOL_WINDOWS = (2, 4, 8, 16)
POOL_WIDTH = D_MODEL
POOL_GROUP_DIM = POOL_WIDTH // POOL_GROUPS
D_MIX = DN_WIDTH + POOL_WIDTH
PROJ_SIZES = (3 * DN_WIDTH, 4 * DN_HEADS, DN_WIDTH, POOL_WIDTH, POOL_WIDTH)
D_PROJ = sum(PROJ_SIZES)
PROJ_SPLITS = tuple(int(s) for s in np.cumsum(PROJ_SIZES)[:-1])
EPS = 1e-6

kernel_name = "hymba_gated_deltanet_multiscale_pool_prefix_ctx"


def _rms_norm(h, gain):
    hf = h.astype(jnp.float32)
    hf = hf * lax.rsqrt(jnp.mean(hf * hf, axis=-1, keepdims=True) + EPS)
    return hf.astype(h.dtype) * gain


def _l2_norm(h):
    hf = h.astype(jnp.float32)
    hf = hf * lax.rsqrt(jnp.sum(hf * hf, axis=-1, keepdims=True) + EPS)
    return hf.astype(h.dtype)


def _short_conv(u, w):
    k = w.shape[0]
    return lax.conv_general_dilated(
        u, w[:, None, :], window_strides=(1,), padding=[(k // 2, k // 2)],
        dimension_numbers=('NWC', 'WIO', 'NWC'), feature_group_count=u.shape[-1])


def _gated_delta_rule(q, k, v, g, beta, s0):
    out_dtype = v.dtype
    q, k, v, g, beta = (t.astype(jnp.float32) for t in (q, k, v, g, beta))
    bsz, nh, length, dk = q.shape
    dv = v.shape[-1]
    n = length // CHUNK
    q = q * dk ** -0.5
    q, k, v = (t.reshape(bsz, nh, n, CHUNK, -1) for t in (q, k, v))
    g = jnp.cumsum(g.reshape(bsz, nh, n, CHUNK), axis=-1)
    beta = beta.reshape(bsz, nh, n, CHUNK, 1)
    incl = jnp.tril(jnp.ones((CHUNK, CHUNK), bool))
    strict = jnp.tril(jnp.ones((CHUNK, CHUNK), bool), -1)
    decay = jnp.exp(jnp.where(incl, g[..., :, None] - g[..., None, :], -jnp.inf))
    kb = k * beta
    a_mat = jnp.where(strict, jnp.einsum('bhncd,bhnsd->bhncs', kb, k) * decay, 0.0)
    eye = jnp.eye(CHUNK, dtype=jnp.float32)
    rhs = jnp.concatenate([v * beta, kb * jnp.exp(g)[..., None]], axis=-1)
    sol = lax.linalg.triangular_solve(eye + a_mat, rhs, left_side=True, lower=True,
                                      unit_diagonal=True)
    u_c, w_c = sol[..., :dv], sol[..., dv:]
    qk = jnp.einsum('bhncd,bhnsd->bhncs', q, k) * decay
    q_dec = q * jnp.exp(g)[..., None]
    g_last = g[..., -1]
    k_tail = k * jnp.exp(g_last[..., None] - g)[..., None]
    xs = tuple(jnp.moveaxis(t, 2, 0) for t in (u_c, w_c, qk, q_dec, k_tail, g_last))

    def step(state, inp):
        u_i, w_i, qk_i, q_i, k_i, gl_i = inp
        v_new = u_i - jnp.einsum('bhcd,bhdv->bhcv', w_i, state)
        o_i = (jnp.einsum('bhcd,bhdv->bhcv', q_i, state)
               + jnp.einsum('bhcs,bhsv->bhcv', qk_i, v_new))
        state = state * jnp.exp(gl_i)[..., None, None] + jnp.einsum('bhcd,bhcv->bhdv', k_i, v_new)
        return state, o_i

    s_final, o = lax.scan(step, s0.astype(jnp.float32), xs)
    o = jnp.moveaxis(o, 0, 2).reshape(bsz, nh, length, dv)
    return o.astype(out_dtype), s_final


def _window_bounds(n, w):
    idx = jnp.arange(n)
    lo = jnp.clip(idx - w // 2, 0, n)
    hi = jnp.clip(idx - w // 2 + w, 0, n)
    return lo, hi


def _pool_grid(u, w):
    bsz, length, ch = u.shape
    rows = length // GRID_W
    grid = u.reshape(bsz, rows, GRID_W, ch)
    sat = jnp.pad(jnp.cumsum(jnp.cumsum(grid, axis=1), axis=2), ((0, 0), (1, 0), (1, 0), (0, 0)))
    lr, hr = _window_bounds(rows, w)
    lc, hc = _window_bounds(GRID_W, w)
    s_hr, s_lr = sat[:, hr], sat[:, lr]
    box = s_hr[:, :, hc] - s_lr[:, :, hc] - s_hr[:, :, lc] + s_lr[:, :, lc]
    cnt = ((hr - lr)[:, None] * (hc - lc)[None, :]).astype(u.dtype)
    return (box / cnt[None, :, :, None]).reshape(bsz, length, ch)


def _pool_seq(u, w):
    length = u.shape[1]
    cs = jnp.pad(jnp.cumsum(u, axis=1), ((0, 0), (1, 0), (0, 0)))
    lo, hi = _window_bounds(length, w)
    return (cs[:, hi] - cs[:, lo]) / (hi - lo).astype(u.dtype)[None, :, None]


def _pool_mixer(u, grid, pool_w, pool_scale):
    bsz, length, _ = u.shape
    pool = _pool_grid if grid else _pool_seq
    groups = jnp.split(u.astype(jnp.float32), POOL_GROUPS, axis=-1)
    d = jnp.stack([pool(gu, w) - gu for gu, w in zip(groups, POOL_WINDOWS)], axis=2)
    y = jnp.einsum('blgc,gce->blge', d.astype(u.dtype), pool_w).reshape(bsz, length, POOL_WIDTH)
    return y * pool_scale


def _mix(h, shift, scale, grid, s0_f, s0_b, gain, w_in, conv_w, a_log, dt_bias, dn_gain,
         pool_w, pool_scale):
    bsz, length, _ = h.shape
    hn = _rms_norm(h, gain) * (1 + scale) + shift
    proj = jnp.einsum('bld,dp->blp', hn, w_in)
    qkv, ab, z_dn, u_pool, z_pool = jnp.split(proj, PROJ_SPLITS, axis=-1)
    qkv = jax.nn.silu(_short_conv(qkv, conv_w))
    q, k, v = (t.reshape(bsz, length, DN_HEADS, DN_HEAD_DIM).transpose(0, 2, 1, 3)
               for t in jnp.split(qkv, 3, axis=-1))
    q, k = _l2_norm(q), _l2_norm(k)
    ab = ab.astype(jnp.float32).reshape(bsz, length, 2, 2, DN_HEADS).transpose(2, 3, 0, 4, 1)
    g = -jnp.exp(a_log.astype(jnp.float32))[:, None, :, None] * jax.nn.softplus(
        ab[0] + dt_bias.astype(jnp.float32)[:, None, :, None])
    beta = jax.nn.sigmoid(ab[1])
    o_f, s_f = _gated_delta_rule(q, k, v, g[0], beta[0], s0_f)
    o_b, s_b = _gated_delta_rule(jnp.flip(q, 2), jnp.flip(k, 2), jnp.flip(v, 2),
                                 jnp.flip(g[1], -1), jnp.flip(beta[1], -1), s0_b)
    o = _rms_norm(o_f + jnp.flip(o_b, 2), dn_gain)
    o = o.transpose(0, 2, 1, 3).reshape(bsz, length, DN_WIDTH) * jax.nn.silu(z_dn)
    y_pool = _pool_mixer(u_pool, grid, pool_w, pool_scale) * jax.nn.silu(z_pool)
    return jnp.concatenate([o, y_pool], axis=-1), s_f, s_b


def setup_inputs(seed: int = 0) -> dict:
    key = jax.random.key(seed)
    ks = jax.random.split(key, 20)
    f32 = jnp.float32
    nrm = lambda k, shape, s: jax.random.normal(k, shape, f32) * s
    a_init = jnp.log(jax.random.uniform(ks[10], (DEPTH, 2, DN_HEADS), f32, 1.0, 16.0))
    dt = jnp.exp(jax.random.uniform(ks[11], (DEPTH, 2, DN_HEADS), f32,
                                    float(np.log(1e-3)), float(np.log(1e-1))))
    dt_bias = dt + jnp.log(-jnp.expm1(-dt))
    return {
        "x": nrm(ks[0], (BATCH, SEQ, D_MODEL), 1.0),
        "c": nrm(ks[1], (BATCH, D_MODEL), 1.0),
        "ctx": nrm(ks[2], (BATCH, CTX_LEN, D_MODEL), 1.0),
        "c_ctx": nrm(ks[3], (D_MODEL,), 1.0),
        "w_mod": nrm(ks[4], (DEPTH, D_MODEL, 3 * D_MODEL), D_MODEL ** -0.5),
        "b_mod": nrm(ks[5], (DEPTH, 3 * D_MODEL), 0.01),
        "norm_gain": 1.0 + nrm(ks[6], (DEPTH, D_MODEL), 0.1),
        "w_in": nrm(ks[7], (DEPTH, D_MODEL, D_PROJ), D_MODEL ** -0.5),
        "conv_w": nrm(ks[8], (DEPTH, SHORT_CONV, 3 * DN_WIDTH), SHORT_CONV ** -0.5),
        "a_log": a_init,
        "dt_bias": dt_bias,
        "dn_norm_gain": 1.0 + nrm(ks[12], (DEPTH, DN_HEAD_DIM), 0.1),
        "pool_w": nrm(ks[13], (DEPTH, POOL_GROUPS, POOL_GROUP_DIM, POOL_GROUP_DIM),
                       POOL_GROUP_DIM ** -0.5),
        "pool_scale": 1.0 + nrm(ks[14], (DEPTH, POOL_WIDTH), 0.1),
        "w_out": nrm(ks[15], (DEPTH, D_MIX, D_MODEL), D_MIX ** -0.5),
        "final_gain": 1.0 + nrm(ks[16], (D_MODEL,), 0.1),
    }


def reference(x, c, ctx, c_ctx, w_mod, b_mod, norm_gain, w_in, conv_w, a_log, dt_bias,
              dn_norm_gain, pool_w, pool_scale, w_out, final_gain):
    bsz = x.shape[0]
    silu_c = jax.nn.silu(c)
    silu_cc = jax.nn.silu(c_ctx)
    s_zero = jnp.zeros((bsz, DN_HEADS, DN_HEAD_DIM, DN_HEAD_DIM), jnp.float32)
    for layer in range(DEPTH):
        mod_x = (silu_c @ w_mod[layer] + b_mod[layer])[:, None, :]
        shift_x, scale_x, gate_x = jnp.split(mod_x, 3, axis=-1)
        mod_c = silu_cc @ w_mod[layer] + b_mod[layer]
        shift_c, scale_c, gate_c = jnp.split(mod_c, 3, axis=-1)
        params = (norm_gain[layer], w_in[layer], conv_w[layer], a_log[layer], dt_bias[layer],
                  dn_norm_gain[layer], pool_w[layer], pool_scale[layer])
        y_ctx, s_f, s_b = _mix(ctx, shift_c, scale_c, False, s_zero, s_zero, *params)
        y_x, _, _ = _mix(x, shift_x, scale_x, True, s_f, s_b, *params)
        x = x + gate_x * jnp.einsum('blm,md->bld', y_x, w_out[layer])
        if layer < DEPTH - 1:
            ctx = ctx + gate_c * jnp.einsum('blm,md->bld', y_ctx, w_out[layer])
    return _rms_norm(x, final_gain)
```

```python
import functools

import jax
import jax.numpy as jnp
import numpy as np
from jax import lax
from jax.experimental import pallas as pl
from jax.experimental.pallas import tpu as pltpu

GRID_W = 64
DN_HEADS = 8
DN_HEAD_DIM = 128
DN_WIDTH = DN_HEADS * DN_HEAD_DIM
SHORT_CONV = 5
CHUNK = 64
POOL_GROUPS = 4
POOL_WINDOWS = (2, 4, 8, 16)
EPS = 1e-6

LANES = 128
SUBLANES = 8
HEAD_GROUP = 4
VMEM_LIMIT = 56 * 1024 * 1024

F32 = jnp.float32
BF16 = jnp.bfloat16
HI = lax.Precision.HIGHEST


def _sigmoid(x):
    return 1.0 / (1.0 + jnp.exp(-x))


def _silu(x):
    return x * _sigmoid(x)


def _softplus(x):
    return jnp.maximum(x, 0.0) + jnp.log1p(jnp.exp(-jnp.abs(x)))


def _params(*sem):
    return pltpu.CompilerParams(dimension_semantics=sem, vmem_limit_bytes=VMEM_LIMIT)


def _mod_kernel(c_ref, w_ref, b_ref, o_ref):
    o_ref[...] = jnp.dot(_silu(c_ref[...]), w_ref[...], precision=HI,
                         preferred_element_type=F32) + b_ref[...]


def _mod_rows(cc, w_mod, b_mod):
    rows, d = cc.shape
    n = w_mod.shape[1]
    tn = 1024
    return pl.pallas_call(
        _mod_kernel,
        grid=(n // tn,),
        in_specs=[pl.BlockSpec((rows, d), lambda j: (0, 0)),
                  pl.BlockSpec((d, tn), lambda j: (0, j)),
                  pl.BlockSpec((1, tn), lambda j: (0, j))],
        out_specs=pl.BlockSpec((rows, tn), lambda j: (0, j)),
        out_shape=jax.ShapeDtypeStruct((rows, n), F32),
        compiler_params=_params("arbitrary"),
        name="mod_rows",
    )(cc, w_mod, b_mod.reshape(1, n))


def _in_proj_kernel(x_ref, shift_ref, scale_ref, gain_ref, *refs, widths, tn):
    n_seg = len(widths)
    w_refs, o_refs = refs[:n_seg], refs[n_seg:]
    x = x_ref[0]
    d = x.shape[-1]
    ms = jnp.sum(x * x, axis=-1, keepdims=True) * (1.0 / d)
    hn = (x * lax.rsqrt(ms + EPS)) * gain_ref[...]
    hn = hn * (1.0 + scale_ref[0]) + shift_ref[0]
    hb = hn.astype(BF16)
    for w_ref, o_ref, width in zip(w_refs, o_refs, widths):
        step = min(tn, width)
        for n0 in range(0, width, step):
            o_ref[0, :, n0:n0 + step] = jnp.dot(hb, w_ref[:, n0:n0 + step],
                                                preferred_element_type=F32)


def _in_proj(h, shift, scale, gain, weights, per_batch, tm):
    bsz, length, d = h.shape
    widths = tuple(int(w.shape[1]) for w in weights)
    mod_map = (lambda b, j: (b, 0, 0)) if per_batch else (lambda b, j: (0, 0, 0))
    in_specs = [pl.BlockSpec((1, tm, d), lambda b, j: (b, j, 0)),
                pl.BlockSpec((1, 1, d), mod_map),
                pl.BlockSpec((1, 1, d), mod_map),
                pl.BlockSpec((1, d), lambda b, j: (0, 0))]
    in_specs += [pl.BlockSpec((d, w), lambda b, j: (0, 0)) for w in widths]
    out_specs = [pl.BlockSpec((1, tm, w), lambda b, j: (b, j, 0)) for w in widths]
    out_shape = [jax.ShapeDtypeStruct((bsz, length, w), F32) for w in widths]
    return pl.pallas_call(
        functools.partial(_in_proj_kernel, widths=widths, tn=512),
        grid=(bsz, length // tm),
        in_specs=in_specs,
        out_specs=out_specs,
        out_shape=out_shape,
        compiler_params=_params("parallel", "arbitrary"),
        name="in_proj",
    )(h, shift, scale, gain.reshape(1, d), *weights)


def _conv_kernel(main_ref, prev_ref, next_ref, cw_ref, ab_ref, alog_ref, dtb_ref,
                 qkv_ref, gates_ref, ext_ref, *, tl, q_scale):
    j = pl.program_id(1)
    nj = pl.num_programs(1)
    halo = SUBLANES
    pad = SHORT_CONV // 2
    not_first = (j > 0).astype(F32)
    not_last = (j < nj - 1).astype(F32)
    ext_ref[0:halo, :] = prev_ref[0] * not_first
    ext_ref[halo:halo + tl, :] = main_ref[0]
    ext_ref[halo + tl:halo + tl + halo, :] = next_ref[0] * not_last
    n_col = main_ref.shape[-1] // LANES
    rb = 64
    for cb in range(n_col):
        cs = slice(cb * LANES, (cb + 1) * LANES)
        taps = [cw_ref[t:t + 1, cs] for t in range(SHORT_CONV)]
        for r0 in range(0, tl, rb):
            acc = None
            for t in range(SHORT_CONV):
                start = halo - pad + t + r0
                term = ext_ref[start:start + rb, cs] * taps[t]
                acc = term if acc is None else acc + term
            y = _silu(acc)
            if cb < 2 * DN_HEADS:
                ss = jnp.sum(y * y, axis=-1, keepdims=True)
                y = y * lax.rsqrt(ss + EPS)
                if cb < DN_HEADS:
                    y = y * q_scale
            qkv_ref[0, r0:r0 + rb, cs] = y
    ab = ab_ref[0]
    g = -jnp.exp(alog_ref[...]) * _softplus(ab + dtb_ref[...])
    beta = _sigmoid(ab)
    lane = lax.broadcasted_iota(jnp.int32, ab.shape, 1)
    gates_ref[0] = jnp.where(lane < 2 * DN_HEADS, g, beta)


def _conv_gates(qkv, ab, conv_w, alog_row, dtb_row, tl):
    bsz, length, c = qkv.shape
    halo = SUBLANES
    nblk8 = length // halo
    per = tl // halo
    return pl.pallas_call(
        functools.partial(_conv_kernel, tl=tl, q_scale=float(DN_HEAD_DIM) ** -0.5),
        grid=(bsz, length // tl),
        in_specs=[pl.BlockSpec((1, tl, c), lambda b, j: (b, j, 0)),
                  pl.BlockSpec((1, halo, c), lambda b, j: (b, jnp.maximum(j * per - 1, 0), 0)),
                  pl.BlockSpec((1, halo, c), lambda b, j: (b, jnp.minimum((j + 1) * per, nblk8 - 1), 0)),
                  pl.BlockSpec((SUBLANES, c), lambda b, j: (0, 0)),
                  pl.BlockSpec((1, tl, LANES), lambda b, j: (b, j, 0)),
                  pl.BlockSpec((1, LANES), lambda b, j: (0, 0)),
                  pl.BlockSpec((1, LANES), lambda b, j: (0, 0))],
        out_specs=[pl.BlockSpec((1, tl, c), lambda b, j: (b, j, 0)),
                   pl.BlockSpec((1, tl, LANES), lambda b, j: (b, j, 0))],
        out_shape=[jax.ShapeDtypeStruct((bsz, length, c), F32),
                   jax.ShapeDtypeStruct((bsz, length, LANES), F32)],
        scratch_shapes=[pltpu.VMEM((tl + 2 * halo, c), F32)],
        compiler_params=_params("parallel", "arbitrary"),
        name="conv_gates",
    )(qkv, qkv, qkv, conv_w, ab, alog_row, dtb_row)


def _tri_inverse(a, lower, prec):
    c = a.shape[0]
    row = lax.broadcasted_iota(jnp.int32, (c, c), 0)
    col = lax.broadcasted_iota(jnp.int32, (c, c), 1)
    x = (row == col).astype(F32)
    s = 1
    while s < c:
        rb, cb = row // s, col // s
        if lower:
            off = (rb % 2 == 1) & (cb == rb - 1)
        else:
            off = (rb % 2 == 0) & (cb == rb + 1)
        l_k = jnp.where(off, a, 0.0)
        p = jnp.dot(l_k, x, precision=prec, preferred_element_type=F32)
        x = x - jnp.dot(x, p, precision=prec, preferred_element_type=F32)
        s *= 2
    return x


def _delta_unit(q, k, v, g_col, beta_col, state, lower, prec):
    c = q.shape[0]
    row = lax.broadcasted_iota(jnp.int32, (c, c), 0)
    col = lax.broadcasted_iota(jnp.int32, (c, c), 1)
    incl = (row >= col) if lower else (row <= col)
    eye = row == col
    cum = jnp.dot(incl.astype(F32), jnp.broadcast_to(g_col, (c, LANES)), precision=HI,
                  preferred_element_type=F32)
    cum_c = cum[:, 0:c]
    cum_row = jnp.sum(jnp.where(eye, cum_c, 0.0), axis=0, keepdims=True)
    decay = jnp.exp(jnp.where(incl, cum_c - cum_row, -1e30))
    beta = jnp.broadcast_to(beta_col, (c, LANES))
    kb = k * beta
    e_cum = jnp.exp(cum)
    m1 = lax.dot_general(jnp.concatenate([kb, q], axis=0), k, (((1,), (1,)), ((), ())),
                         precision=prec, preferred_element_type=F32)
    a_mat = jnp.where(eye, 0.0, m1[0:c] * decay)
    qk = m1[c:2 * c] * decay
    t_inv = _tri_inverse(a_mat, lower, prec)
    rhs = jnp.concatenate([v * beta, kb * e_cum], axis=1)
    sol = jnp.dot(t_inv, rhs, precision=prec, preferred_element_type=F32)
    dv = v.shape[1]
    u_c, w_c = sol[:, 0:dv], sol[:, dv:]
    q_dec = q * e_cum
    last = c - 1 if lower else 0
    cum_last = cum[last:last + 1, :]
    k_tail = k * jnp.exp(cum_last - cum)
    ws = jnp.dot(jnp.concatenate([w_c, q_dec], axis=0), state, precision=prec,
                 preferred_element_type=F32)
    v_new = u_c - ws[0:c]
    o = ws[c:2 * c] + jnp.dot(qk, v_new, precision=prec, preferred_element_type=F32)
    new_state = state * jnp.exp(cum_last) + lax.dot_general(
        k_tail, v_new, (((0,), (0,)), ((), ())), precision=prec, preferred_element_type=F32)
    return o, new_state


def _delta_kernel(qf_ref, kf_ref, vf_ref, gf_ref, qb_ref, kb_ref, vb_ref, gb_ref,
                  s0f_ref, s0b_ref, of_ref, ob_ref, sf_ref, sb_ref, state_ref, *, prec):
    i = pl.program_id(2)
    ni = pl.num_programs(2)
    hg = HEAD_GROUP

    @pl.when(i == 0)
    def _():
        state_ref[0:hg] = s0f_ref[0]
        state_ref[hg:2 * hg] = s0b_ref[0]

    for d, (q_ref, k_ref, v_ref, g_ref, o_ref) in enumerate(
            ((qf_ref, kf_ref, vf_ref, gf_ref, of_ref), (qb_ref, kb_ref, vb_ref, gb_ref, ob_ref))):
        gates = g_ref[0, 0]
        for hh in range(hg):
            cs = slice(hh * DN_HEAD_DIM, (hh + 1) * DN_HEAD_DIM)
            gi = d * hg + hh
            bi = 2 * hg + d * hg + hh
            o, s_new = _delta_unit(q_ref[0, :, cs], k_ref[0, :, cs], v_ref[0, :, cs],
                                   gates[:, gi:gi + 1], gates[:, bi:bi + 1],
                                   state_ref[d * hg + hh], lower=(d == 0), prec=prec)
            o_ref[0, :, cs] = o
            state_ref[d * hg + hh] = s_new

    @pl.when(i == ni - 1)
    def _():
        sf_ref[0] = state_ref[0:hg]
        sb_ref[0] = state_ref[hg:2 * hg]


def _delta_rule(qkv, gates_hg, s0f, s0b, prec):
    bsz, length, _ = qkv.shape
    nc = length // CHUNK
    hgw = HEAD_GROUP * DN_HEAD_DIM
    n_hg = DN_HEADS // HEAD_GROUP
    blk = (1, CHUNK, hgw)

    def spec(col0, rev):
        if rev:
            return pl.BlockSpec(blk, lambda b, h, i: (b, nc - 1 - i, col0 + h))
        return pl.BlockSpec(blk, lambda b, h, i: (b, i, col0 + h))

    def gspec(rev):
        if rev:
            return pl.BlockSpec((1, 1, CHUNK, 16), lambda b, h, i: (b, h, nc - 1 - i, 0))
        return pl.BlockSpec((1, 1, CHUNK, 16), lambda b, h, i: (b, h, i, 0))

    sspec = pl.BlockSpec((1, HEAD_GROUP, DN_HEAD_DIM, DN_HEAD_DIM), lambda b, h, i: (b, h, 0, 0))
    o_shape = jax.ShapeDtypeStruct((bsz, length, DN_WIDTH), F32)
    s_shape = jax.ShapeDtypeStruct((bsz, DN_HEADS, DN_HEAD_DIM, DN_HEAD_DIM), F32)
    return pl.pallas_call(
        functools.partial(_delta_kernel, prec=prec),
        grid=(bsz, n_hg, nc),
        in_specs=[spec(0, False), spec(n_hg, False), spec(2 * n_hg, False), gspec(False),
                  spec(0, True), spec(n_hg, True), spec(2 * n_hg, True), gspec(True),
                  sspec, sspec],
        out_specs=[pl.BlockSpec(blk, lambda b, h, i: (b, i, h)),
                   pl.BlockSpec(blk, lambda b, h, i: (b, nc - 1 - i, h)),
                   sspec, sspec],
        out_shape=[o_shape, o_shape, s_shape, s_shape],
        scratch_shapes=[pltpu.VMEM((2 * HEAD_GROUP, DN_HEAD_DIM, DN_HEAD_DIM), F32)],
        compiler_params=_params("parallel", "parallel", "arbitrary"),
        name="delta_rule",
    )(qkv, qkv, qkv, gates_hg, qkv, qkv, qkv, gates_hg, s0f, s0b)


def _pool_kernel(u_ref, z_ref, w_ref, scale_ref, y_ref, cs_ref, d_ref, *, win, rows):
    half = win // 2
    gw = GRID_W
    ch = u_ref.shape[-1]
    cidx = lax.broadcasted_iota(jnp.int32, (gw, ch), 0)
    col_cnt = (jnp.minimum(cidx - half + win, gw) - jnp.maximum(cidx - half, 0)).astype(F32)
    zeros = jnp.zeros((gw, ch), F32)
    for r in range(half):
        cs_ref[r * gw:(r + 1) * gw, :] = zeros
    for r in range(half - 1):
        cs_ref[(half + rows + r) * gw:(half + rows + r + 1) * gw, :] = zeros

    def col_pass(r, carry):
        start = pl.multiple_of(r * gw, gw)
        tile = u_ref[0, pl.ds(start, gw), :]
        acc = None
        for dc in range(-half, half):
            if dc == 0:
                term = tile
            else:
                rolled = pltpu.roll(tile, (-dc) % gw, 0)
                ok = (cidx + dc >= 0) & (cidx + dc < gw)
                term = jnp.where(ok, rolled, 0.0)
            acc = term if acc is None else acc + term
        cs_ref[pl.ds(pl.multiple_of((r + half) * gw, gw), gw), :] = acc
        return carry

    lax.fori_loop(0, rows, col_pass, 0)

    def row_pass(r, carry):
        acc = None
        for k in range(win):
            term = cs_ref[pl.ds(pl.multiple_of((r + k) * gw, gw), gw), :]
            acc = term if acc is None else acc + term
        row_cnt = (jnp.minimum(r - half + win, rows) - jnp.maximum(r - half, 0)).astype(F32)
        start = pl.multiple_of(r * gw, gw)
        d_ref[pl.ds(start, gw), :] = acc / (col_cnt * row_cnt) - u_ref[0, pl.ds(start, gw), :]
        return carry

    lax.fori_loop(0, rows, row_pass, 0)

    tm = min(512, rows * gw)
    for m0 in range(0, rows * gw, tm):
        y = jnp.dot(d_ref[m0:m0 + tm, :].astype(BF16), w_ref[0], preferred_element_type=F32)
        y_ref[0, m0:m0 + tm, :] = y * scale_ref[...] * _silu(z_ref[0, m0:m0 + tm, :])


def _pool_group(u_pool, z_pool, pool_w_bf16, pool_scale_row, g):
    bsz, length, width = u_pool.shape
    ch = width // POOL_GROUPS
    rows = length // GRID_W
    win = POOL_WINDOWS[g]
    return pl.pallas_call(
        functools.partial(_pool_kernel, win=win, rows=rows),
        grid=(bsz,),
        in_specs=[pl.BlockSpec((1, length, ch), lambda b: (b, 0, g)),
                  pl.BlockSpec((1, length, ch), lambda b: (b, 0, g)),
                  pl.BlockSpec((1, ch, ch), lambda b: (g, 0, 0)),
                  pl.BlockSpec((1, ch), lambda b: (0, g))],
        out_specs=pl.BlockSpec((1, length, ch), lambda b: (b, 0, 0)),
        out_shape=jax.ShapeDtypeStruct((bsz, length, ch), F32),
        scratch_shapes=[pltpu.VMEM(((rows + win) * GRID_W, ch), F32),
                        pltpu.VMEM((length, ch), F32)],
        compiler_params=_params("parallel"),
        name=f"pool_w{win}",
    )(u_pool, z_pool, pool_w_bf16, pool_scale_row)


def _out_proj_kernel(of_ref, ob_ref, z_ref, y0_ref, y1_ref, y2_ref, y3_ref, x_ref, gate_ref,
                     dng_ref, fg_ref, w_ref, out_ref):
    o = of_ref[0] + ob_ref[0]
    z = z_ref[0]
    parts = []
    for h in range(DN_HEADS):
        cs = slice(h * DN_HEAD_DIM, (h + 1) * DN_HEAD_DIM)
        oh = o[:, cs]
        ms = jnp.sum(oh * oh, axis=-1, keepdims=True) * (1.0 / DN_HEAD_DIM)
        oh = (oh * lax.rsqrt(ms + EPS)) * dng_ref[...]
        parts.append((oh * _silu(z[:, cs])).astype(BF16))
    acc = jnp.dot(jnp.concatenate(parts, axis=1), w_ref[0:DN_WIDTH, :], preferred_element_type=F32)
    ch = y0_ref.shape[-1]
    for g, y_ref in enumerate((y0_ref, y1_ref, y2_ref, y3_ref)):
        r0 = DN_WIDTH + g * ch
        acc = acc + jnp.dot(y_ref[0].astype(BF16), w_ref[r0:r0 + ch, :], preferred_element_type=F32)
    xn = x_ref[0] + gate_ref[0] * acc
    d = xn.shape[-1]
    ms = jnp.sum(xn * xn, axis=-1, keepdims=True) * (1.0 / d)
    out_ref[0] = (xn * lax.rsqrt(ms + EPS)) * fg_ref[...]


def _out_proj(o_f, o_b, z_dn, y_pools, x, gate, dn_gain, final_gain, w_out_bf16, tm):
    bsz, length, d = x.shape
    ch = y_pools[0].shape[-1]
    tok = lambda w: pl.BlockSpec((1, tm, w), lambda b, j: (b, j, 0))
    const = lambda shape: pl.BlockSpec(shape, lambda b, j: tuple(0 for _ in shape))
    return pl.pallas_call(
        _out_proj_kernel,
        grid=(bsz, length // tm),
        in_specs=[tok(DN_WIDTH), tok(DN_WIDTH), tok(DN_WIDTH), tok(ch), tok(ch), tok(ch), tok(ch),
                  tok(d), pl.BlockSpec((1, 1, d), lambda b, j: (b, 0, 0)),
                  const((1, DN_HEAD_DIM)), const((1, d)), const(w_out_bf16.shape)],
        out_specs=tok(d),
        out_shape=jax.ShapeDtypeStruct((bsz, length, d), F32),
        compiler_params=_params("parallel", "arbitrary"),
        name="out_proj",
    )(o_f, o_b, z_dn, *y_pools, x, gate, dn_gain.reshape(1, DN_HEAD_DIM),
      final_gain.reshape(1, d), w_out_bf16)


def _gates_by_head_group(gates):
    bsz, length, _ = gates.shape
    n_hg = DN_HEADS // HEAD_GROUP
    g = gates[:, :, :4 * DN_HEADS].reshape(bsz, length, 2, 2, n_hg, HEAD_GROUP)
    return g.transpose(0, 4, 1, 2, 3, 5).reshape(bsz, n_hg, length, 4 * HEAD_GROUP)


def _pad_lanes(row, n=LANES):
    row = row.reshape(1, -1).astype(F32)
    return jnp.pad(row, ((0, 0), (0, n - row.shape[1])))


def _layer(x, ctx, shift_x, scale_x, gate_x, shift_c, scale_c, norm_gain, w_in, conv_w, a_log,
           dt_bias, dn_norm_gain, pool_w, pool_scale, w_out, final_gain, prec):
    bsz, length, d = x.shape
    w3 = 3 * DN_WIDTH
    nab = 4 * DN_HEADS
    w_qkv = w_in[:, :w3].astype(BF16)
    w_ab = jnp.pad(w_in[:, w3:w3 + nab], ((0, 0), (0, LANES - nab))).astype(BF16)
    rest = w_in[:, w3 + nab:].astype(BF16)
    w_zdn, w_up, w_zp = rest[:, :DN_WIDTH], rest[:, DN_WIDTH:DN_WIDTH + d], rest[:, DN_WIDTH + d:]
    cw = jnp.pad(conv_w, ((0, SUBLANES - SHORT_CONV), (0, 0)))
    alog_row, dtb_row = _pad_lanes(a_log), _pad_lanes(dt_bias)
    zero_state = jnp.zeros((bsz, DN_HEADS, DN_HEAD_DIM, DN_HEAD_DIM), F32)

    c_len = ctx.shape[1]
    qkv_c, ab_c = _in_proj(ctx, shift_c, scale_c, norm_gain, [w_qkv, w_ab], False, min(256, c_len))
    qkv_c, gates_c = _conv_gates(qkv_c, ab_c, cw, alog_row, dtb_row, min(256, c_len))
    _, _, s_f, s_b = _delta_rule(qkv_c, _gates_by_head_group(gates_c), zero_state, zero_state, prec)

    qkv, ab, z_dn, u_pool, z_pool = _in_proj(x, shift_x, scale_x, norm_gain,
                                             [w_qkv, w_ab, w_zdn, w_up, w_zp], True, 256)
    qkv, gates = _conv_gates(qkv, ab, cw, alog_row, dtb_row, 256)
    o_f, o_b, _, _ = _delta_rule(qkv, _gates_by_head_group(gates), s_f, s_b, prec)
    pw = pool_w.astype(BF16)
    ps = pool_scale.reshape(1, -1)
    y_pools = [_pool_group(u_pool, z_pool, pw, ps, g) for g in range(POOL_GROUPS)]
    return _out_proj(o_f, o_b, z_dn, y_pools, x, gate_x, dn_norm_gain, final_gain,
                     w_out.astype(BF16), 256)


def kernel(x, c, ctx, c_ctx, w_mod, b_mod, norm_gain, w_in, conv_w, a_log, dt_bias, dn_norm_gain,
           pool_w, pool_scale, w_out, final_gain):
    bsz, length, d = x.shape
    depth = w_mod.shape[0]
    assert depth == 1, "single trunk layer"
    rows = -(-(bsz + 1) // SUBLANES) * SUBLANES
    cc = jnp.zeros((rows, d), F32).at[:bsz].set(c).at[bsz].set(c_ctx)
    mod = _mod_rows(cc, w_mod[0], b_mod[0])
    shift_x, scale_x, gate_x = (mod[:bsz, k * d:(k + 1) * d].reshape(bsz, 1, d) for k in range(3))
    shift_c, scale_c = (mod[bsz:bsz + 1, k * d:(k + 1) * d].reshape(1, 1, d) for k in range(2))
    return _layer(x, ctx, shift_x, scale_x, gate_x, shift_c, scale_c, norm_gain[0], w_in[0],
                  conv_w[0], a_log[0], dt_bias[0], dn_norm_gain[0], pool_w[0], pool_scale[0],
                  w_out[0], final_gain, HI)
```

```python
import functools

import jax
import jax.numpy as jnp
from jax import lax
from jax.experimental import pallas as pl
from jax.experimental.pallas import tpu as pltpu

GRID_W = 64
DN_HEADS = 8
DN_HEAD_DIM = 128
DN_WIDTH = DN_HEADS * DN_HEAD_DIM
SHORT_CONV = 5
CHUNK = 64
POOL_GROUPS = 4
POOL_WINDOWS = (2, 4, 8, 16)
EPS = 1e-6

LANES = 128
SUBLANES = 8
HEAD_GROUP = 4
VMEM_LIMIT = 56 * 1024 * 1024

F32 = jnp.float32
BF16 = jnp.bfloat16
HI = lax.Precision.HIGHEST


def _sigmoid(x):
    return 1.0 / (1.0 + jnp.exp(-x))


def _silu(x):
    return x * _sigmoid(x)


def _softplus(x):
    return jnp.maximum(x, 0.0) + jnp.log1p(jnp.exp(-jnp.abs(x)))


def _bf(x):
    return x.astype(BF16)


def _params(*sem):
    return pltpu.CompilerParams(dimension_semantics=sem, vmem_limit_bytes=VMEM_LIMIT)


def _mod_kernel(c_ref, w_ref, b_ref, o_ref):
    o_ref[...] = jnp.dot(_silu(c_ref[...]), w_ref[...], precision=HI,
                         preferred_element_type=F32) + b_ref[...]


def _mod_rows(cc, w_mod, b_mod):
    rows, d = cc.shape
    n = w_mod.shape[1]
    tn = 1024
    return pl.pallas_call(
        _mod_kernel,
        grid=(n // tn,),
        in_specs=[pl.BlockSpec((rows, d), lambda j: (0, 0)),
                  pl.BlockSpec((d, tn), lambda j: (0, j)),
                  pl.BlockSpec((1, tn), lambda j: (0, j))],
        out_specs=pl.BlockSpec((rows, tn), lambda j: (0, j)),
        out_shape=jax.ShapeDtypeStruct((rows, n), F32),
        compiler_params=_params("arbitrary"),
        name="mod_rows",
    )(cc, w_mod, b_mod.reshape(1, n))


def _in_proj_kernel(x_ref, shift_ref, scale_ref, gain_ref, *refs, widths, tn):
    n_seg = len(widths)
    w_refs, o_refs = refs[:n_seg], refs[n_seg:]
    x = x_ref[0]
    d = x.shape[-1]
    ms = jnp.sum(x * x, axis=-1, keepdims=True) * (1.0 / d)
    hn = (x * lax.rsqrt(ms + EPS)) * gain_ref[...]
    hn = hn * (1.0 + scale_ref[0]) + shift_ref[0]
    hb = _bf(hn)
    for w_ref, o_ref, width in zip(w_refs, o_refs, widths):
        step = min(tn, width)
        for n0 in range(0, width, step):
            o_ref[0, :, n0:n0 + step] = jnp.dot(hb, w_ref[:, n0:n0 + step],
                                                preferred_element_type=F32)


def _in_proj(h, shift, scale, gain, weights, per_batch, tm):
    bsz, length, d = h.shape
    widths = tuple(int(w.shape[1]) for w in weights)
    mod_map = (lambda b, j: (b, 0, 0)) if per_batch else (lambda b, j: (0, 0, 0))
    in_specs = [pl.BlockSpec((1, tm, d), lambda b, j: (b, j, 0)),
                pl.BlockSpec((1, 1, d), mod_map),
                pl.BlockSpec((1, 1, d), mod_map),
                pl.BlockSpec((1, d), lambda b, j: (0, 0))]
    in_specs += [pl.BlockSpec((d, w), lambda b, j: (0, 0)) for w in widths]
    out_specs = [pl.BlockSpec((1, tm, w), lambda b, j: (b, j, 0)) for w in widths]
    out_shape = [jax.ShapeDtypeStruct((bsz, length, w), F32) for w in widths]
    return pl.pallas_call(
        functools.partial(_in_proj_kernel, widths=widths, tn=512),
        grid=(bsz, length // tm),
        in_specs=in_specs,
        out_specs=out_specs,
        out_shape=out_shape,
        compiler_params=_params("parallel", "arbitrary"),
        name="in_proj",
    )(h, shift, scale, gain.reshape(1, d), *weights)


def _conv_kernel(main_ref, prev_ref, next_ref, cw_ref, ab_ref, alog_ref, dtb_ref,
                 qkv_ref, gates_ref, ext_ref, *, tl, q_scale):
    j = pl.program_id(1)
    nj = pl.num_programs(1)
    halo = SUBLANES
    pad = SHORT_CONV // 2
    not_first = (j > 0).astype(F32)
    not_last = (j < nj - 1).astype(F32)
    ext_ref[0:halo, :] = prev_ref[0] * not_first
    ext_ref[halo:halo + tl, :] = main_ref[0]
    ext_ref[halo + tl:halo + tl + halo, :] = next_ref[0] * not_last
    n_col = main_ref.shape[-1] // LANES
    rb = 64
    for cb in range(n_col):
        cs = slice(cb * LANES, (cb + 1) * LANES)
        taps = [cw_ref[t:t + 1, cs] for t in range(SHORT_CONV)]
        for r0 in range(0, tl, rb):
            acc = None
            for t in range(SHORT_CONV):
                start = halo - pad + t + r0
                term = ext_ref[start:start + rb, cs] * taps[t]
                acc = term if acc is None else acc + term
            y = _silu(acc)
            if cb < 2 * DN_HEADS:
                ss = jnp.sum(y * y, axis=-1, keepdims=True)
                y = y * lax.rsqrt(ss + EPS)
                if cb < DN_HEADS:
                    y = y * q_scale
            qkv_ref[0, r0:r0 + rb, cs] = y
    row = lax.broadcasted_iota(jnp.int32, (CHUNK, CHUNK), 0)
    col = lax.broadcasted_iota(jnp.int32, (CHUNK, CHUNK), 1)
    lower = (row >= col).astype(F32)
    upper = (row <= col).astype(F32)
    lane = lax.broadcasted_iota(jnp.int32, (CHUNK, LANES), 1)
    for r0 in range(0, tl, CHUNK):
        ab = ab_ref[0, r0:r0 + CHUNK, :]
        g = -jnp.exp(alog_ref[...]) * _softplus(ab + dtb_ref[...])
        cum_f = jnp.dot(lower, g, precision=HI, preferred_element_type=F32)
        cum_b = jnp.dot(upper, g, precision=HI, preferred_element_type=F32)
        gates_ref[0, r0:r0 + CHUNK, :] = jnp.where(
            lane < DN_HEADS, cum_f, jnp.where(lane < 2 * DN_HEADS, cum_b, _sigmoid(ab)))


def _conv_gates(qkv, ab, conv_w, alog_row, dtb_row, tl):
    bsz, length, c = qkv.shape
    halo = SUBLANES
    nblk8 = length // halo
    per = tl // halo
    return pl.pallas_call(
        functools.partial(_conv_kernel, tl=tl, q_scale=float(DN_HEAD_DIM) ** -0.5),
        grid=(bsz, length // tl),
        in_specs=[pl.BlockSpec((1, tl, c), lambda b, j: (b, j, 0)),
                  pl.BlockSpec((1, halo, c), lambda b, j: (b, jnp.maximum(j * per - 1, 0), 0)),
                  pl.BlockSpec((1, halo, c), lambda b, j: (b, jnp.minimum((j + 1) * per, nblk8 - 1), 0)),
                  pl.BlockSpec((SUBLANES, c), lambda b, j: (0, 0)),
                  pl.BlockSpec((1, tl, LANES), lambda b, j: (b, j, 0)),
                  pl.BlockSpec((1, LANES), lambda b, j: (0, 0)),
                  pl.BlockSpec((1, LANES), lambda b, j: (0, 0))],
        out_specs=[pl.BlockSpec((1, tl, c), lambda b, j: (b, j, 0)),
                   pl.BlockSpec((1, tl, LANES), lambda b, j: (b, j, 0))],
        out_shape=[jax.ShapeDtypeStruct((bsz, length, c), F32),
                   jax.ShapeDtypeStruct((bsz, length, LANES), F32)],
        scratch_shapes=[pltpu.VMEM((tl + 2 * halo, c), F32)],
        compiler_params=_params("parallel", "arbitrary"),
        name="conv_gates",
    )(qkv, qkv, qkv, conv_w, ab, alog_row, dtb_row)


def _cat_index():
    shape = (CHUNK, HEAD_GROUP * CHUNK)
    row = lax.broadcasted_iota(jnp.int32, shape, 0)
    lane = lax.broadcasted_iota(jnp.int32, shape, 1)
    return row, jnp.bitwise_and(lane, CHUNK - 1), jnp.right_shift(lane, CHUNK.bit_length() - 1)


def _block_diag_dot(l_cat, r_cat, bd_mask):
    w = jnp.where(bd_mask, jnp.concatenate([r_cat] * HEAD_GROUP, axis=0), 0.0)
    return jnp.dot(_bf(l_cat), _bf(w), preferred_element_type=F32)


def _delta_prep(q_ref, k_ref, v_ref, gates, d, lower):
    hg, c, dk = HEAD_GROUP, CHUNK, DN_HEAD_DIM
    row, col, mat = _cat_index()
    incl = (row >= col) if lower else (row <= col)
    eye = row == col
    last = c - 1 if lower else 0
    units = []
    cum_cat = None
    for u in range(hg):
        cs = slice(u * dk, (u + 1) * dk)
        ci = d * hg + u
        bi = 2 * hg + d * hg + u
        cum_col = gates[:, ci:ci + 1]
        cum = jnp.broadcast_to(cum_col, (c, dk))
        beta = jnp.broadcast_to(gates[:, bi:bi + 1], (c, dk))
        q, k, v = q_ref[0, :, cs], k_ref[0, :, cs], v_ref[0, :, cs]
        kb = k * beta
        e_cum = jnp.exp(cum)
        cum_last = cum[last:last + 1, :]
        units.append(dict(
            k=k, kb=kb, q=q,
            rhs=jnp.concatenate([v * beta, kb * e_cum], axis=1),
            q_dec=q * e_cum,
            k_tail=k * jnp.exp(cum_last - cum),
            s_dec=jnp.exp(cum_last)))
        cc = jnp.broadcast_to(cum_col, (c, hg * c))
        cum_cat = cc if u == 0 else jnp.where(mat == u, cc, cum_cat)
    cum_row = jnp.sum(jnp.where(eye, cum_cat, 0.0), axis=0, keepdims=True)
    decay = jnp.exp(jnp.where(incl, cum_cat - cum_row, -1e30))
    return dict(units=units, decay=decay, eye=eye, row=row, col=col, mat=mat, lower=lower)


def _delta_gram(grp):
    c = CHUNK
    units = grp["units"]
    lhs = jnp.concatenate([x for u in units for x in (u["kb"], u["q"])], axis=0)
    kcat = jnp.concatenate([u["k"] for u in units], axis=0)
    m = lax.dot_general(_bf(lhs), _bf(kcat), (((1,), (1,)), ((), ())), preferred_element_type=F32)
    a_cat = qk_cat = None
    for u in range(len(units)):
        a_u, qk_u = m[2 * u * c:(2 * u + 1) * c], m[(2 * u + 1) * c:(2 * u + 2) * c]
        a_cat = a_u if u == 0 else jnp.where(grp["mat"] == u, a_u, a_cat)
        qk_cat = qk_u if u == 0 else jnp.where(grp["mat"] == u, qk_u, qk_cat)
    grp["a"] = jnp.where(grp["eye"], 0.0, a_cat * grp["decay"])
    grp["qk"] = qk_cat * grp["decay"]


def _off_mask(grp, s):
    sh = s.bit_length() - 1
    rb, cb = jnp.right_shift(grp["row"], sh), jnp.right_shift(grp["col"], sh)
    if grp["lower"]:
        return (jnp.bitwise_and(rb, 1) == 1) & (cb == rb - 1)
    return (jnp.bitwise_and(rb, 1) == 0) & (cb == rb + 1)


def _delta_kernel(qf_ref, kf_ref, vf_ref, gf_ref, qb_ref, kb_ref, vb_ref, gb_ref,
                  s0f_ref, s0b_ref, of_ref, ob_ref, sf_ref, sb_ref, state_ref):
    i = pl.program_id(2)
    ni = pl.num_programs(2)
    hg, c, dk = HEAD_GROUP, CHUNK, DN_HEAD_DIM

    @pl.when(i == 0)
    def _():
        state_ref[0:hg] = s0f_ref[0]
        state_ref[hg:2 * hg] = s0b_ref[0]

    groups = [_delta_prep(qf_ref, kf_ref, vf_ref, gf_ref[0, 0], 0, True),
              _delta_prep(qb_ref, kb_ref, vb_ref, gb_ref[0, 0], 1, False)]
    o_refs = (of_ref, ob_ref)
    for grp in groups:
        _delta_gram(grp)

    n = hg * c
    bd_mask = (jnp.right_shift(lax.broadcasted_iota(jnp.int32, (n, n), 0), c.bit_length() - 1)
               == jnp.right_shift(lax.broadcasted_iota(jnp.int32, (n, n), 1), c.bit_length() - 1))
    xs = [grp["eye"].astype(F32) - jnp.where(_off_mask(grp, 1), grp["a"], 0.0) for grp in groups]
    s = 2
    while s < c:
        ls = [jnp.where(_off_mask(grp, s), grp["a"], 0.0) for grp in groups]
        ps = [_block_diag_dot(l, x, bd_mask) for l, x in zip(ls, xs)]
        xs = [x - _block_diag_dot(x, p, bd_mask) for x, p in zip(xs, ps)]
        s *= 2

    sols = [[jnp.dot(_bf(x[:, u * c:(u + 1) * c]), _bf(grp["units"][u]["rhs"]),
                     preferred_element_type=F32) for u in range(hg)]
            for grp, x in zip(groups, xs)]
    wss = [[jnp.dot(_bf(jnp.concatenate([sol[u][:, dk:], grp["units"][u]["q_dec"]], axis=0)),
                    _bf(state_ref[d * hg + u]), preferred_element_type=F32) for u in range(hg)]
           for d, (grp, sol) in enumerate(zip(groups, sols))]
    v_news = [[sol[u][:, 0:dk] - ws[u][0:c] for u in range(hg)] for sol, ws in zip(sols, wss)]
    zero = jnp.zeros((c, dk), BF16)
    for d, (grp, ws, v_new) in enumerate(zip(groups, wss, v_news)):
        vb = [_bf(v) for v in v_new]
        bd_v = jnp.concatenate(
            [jnp.concatenate([vb[u] if t == u else zero for t in range(hg)], axis=1)
             for u in range(hg)], axis=0)
        o2 = jnp.dot(_bf(grp["qk"]), bd_v, preferred_element_type=F32)
        o_refs[d][0] = jnp.concatenate([ws[u][c:2 * c] for u in range(hg)], axis=1) + o2
    for d, (grp, v_new) in enumerate(zip(groups, v_news)):
        for u in range(hg):
            unit = grp["units"][u]
            upd = lax.dot_general(_bf(unit["k_tail"]), _bf(v_new[u]), (((0,), (0,)), ((), ())),
                                  preferred_element_type=F32)
            state_ref[d * hg + u] = state_ref[d * hg + u] * unit["s_dec"] + upd

    @pl.when(i == ni - 1)
    def _():
        sf_ref[0] = state_ref[0:hg]
        sb_ref[0] = state_ref[hg:2 * hg]


def _delta_rule(qkv, gates_hg, s0f, s0b):
    bsz, length, _ = qkv.shape
    nc = length // CHUNK
    hgw = HEAD_GROUP * DN_HEAD_DIM
    n_hg = DN_HEADS // HEAD_GROUP
    blk = (1, CHUNK, hgw)

    def spec(col0, rev):
        if rev:
            return pl.BlockSpec(blk, lambda b, h, i: (b, nc - 1 - i, col0 + h))
        return pl.BlockSpec(blk, lambda b, h, i: (b, i, col0 + h))

    def gspec(rev):
        if rev:
            return pl.BlockSpec((1, 1, CHUNK, 16), lambda b, h, i: (b, h, nc - 1 - i, 0))
        return pl.BlockSpec((1, 1, CHUNK, 16), lambda b, h, i: (b, h, i, 0))

    sspec = pl.BlockSpec((1, HEAD_GROUP, DN_HEAD_DIM, DN_HEAD_DIM), lambda b, h, i: (b, h, 0, 0))
    o_shape = jax.ShapeDtypeStruct((bsz, length, DN_WIDTH), F32)
    s_shape = jax.ShapeDtypeStruct((bsz, DN_HEADS, DN_HEAD_DIM, DN_HEAD_DIM), F32)
    return pl.pallas_call(
        _delta_kernel,
        grid=(bsz, n_hg, nc),
        in_specs=[spec(0, False), spec(n_hg, False), spec(2 * n_hg, False), gspec(False),
                  spec(0, True), spec(n_hg, True), spec(2 * n_hg, True), gspec(True),
                  sspec, sspec],
        out_specs=[pl.BlockSpec(blk, lambda b, h, i: (b, i, h)),
                   pl.BlockSpec(blk, lambda b, h, i: (b, nc - 1 - i, h)),
                   sspec, sspec],
        out_shape=[o_shape, o_shape, s_shape, s_shape],
        scratch_shapes=[pltpu.VMEM((2 * HEAD_GROUP, DN_HEAD_DIM, DN_HEAD_DIM), F32)],
        compiler_params=_params("parallel", "parallel", "arbitrary"),
        name="delta_rule",
    )(qkv, qkv, qkv, gates_hg, qkv, qkv, qkv, gates_hg, s0f, s0b)


def _pool_kernel(u_ref, z_ref, w_ref, scale_ref, y_ref, cs_ref, d_ref, *, win, rows):
    half = win // 2
    gw = GRID_W
    ch = u_ref.shape[-1]
    cidx = lax.broadcasted_iota(jnp.int32, (gw, ch), 0)
    col_cnt = (jnp.minimum(cidx - half + win, gw) - jnp.maximum(cidx - half, 0)).astype(F32)
    zeros = jnp.zeros((gw, ch), F32)
    for r in range(half):
        cs_ref[r * gw:(r + 1) * gw, :] = zeros
    for r in range(half - 1):
        cs_ref[(half + rows + r) * gw:(half + rows + r + 1) * gw, :] = zeros

    def col_pass(r, carry):
        start = pl.multiple_of(r * gw, gw)
        tile = u_ref[0, pl.ds(start, gw), :]
        acc = None
        for dc in range(-half, half):
            if dc == 0:
                term = tile
            else:
                rolled = pltpu.roll(tile, (-dc) % gw, 0)
                ok = (cidx + dc >= 0) & (cidx + dc < gw)
                term = jnp.where(ok, rolled, 0.0)
            acc = term if acc is None else acc + term
        cs_ref[pl.ds(pl.multiple_of((r + half) * gw, gw), gw), :] = acc
        return carry

    lax.fori_loop(0, rows, col_pass, 0)

    def row_pass(r, carry):
        acc = None
        for k in range(win):
            term = cs_ref[pl.ds(pl.multiple_of((r + k) * gw, gw), gw), :]
            acc = term if acc is None else acc + term
        row_cnt = (jnp.minimum(r - half + win, rows) - jnp.maximum(r - half, 0)).astype(F32)
        start = pl.multiple_of(r * gw, gw)
        d_ref[pl.ds(start, gw), :] = acc / (col_cnt * row_cnt) - u_ref[0, pl.ds(start, gw), :]
        return carry

    lax.fori_loop(0, rows, row_pass, 0)

    tm = min(512, rows * gw)
    for m0 in range(0, rows * gw, tm):
        y = jnp.dot(_bf(d_ref[m0:m0 + tm, :]), w_ref[0], preferred_element_type=F32)
        y_ref[0, m0:m0 + tm, :] = y * scale_ref[...] * _silu(z_ref[0, m0:m0 + tm, :])


def _pool_group(u_pool, z_pool, pool_w_bf16, pool_scale_row, g):
    bsz, length, width = u_pool.shape
    ch = width // POOL_GROUPS
    rows = length // GRID_W
    win = POOL_WINDOWS[g]
    return pl.pallas_call(
        functools.partial(_pool_kernel, win=win, rows=rows),
        grid=(bsz,),
        in_specs=[pl.BlockSpec((1, length, ch), lambda b: (b, 0, g)),
                  pl.BlockSpec((1, length, ch), lambda b: (b, 0, g)),
                  pl.BlockSpec((1, ch, ch), lambda b: (g, 0, 0)),
                  pl.BlockSpec((1, ch), lambda b: (0, g))],
        out_specs=pl.BlockSpec((1, length, ch), lambda b: (b, 0, 0)),
        out_shape=jax.ShapeDtypeStruct((bsz, length, ch), F32),
        scratch_shapes=[pltpu.VMEM(((rows + win) * GRID_W, ch), F32),
                        pltpu.VMEM((length, ch), F32)],
        compiler_params=_params("parallel"),
        name=f"pool_w{win}",
    )(u_pool, z_pool, pool_w_bf16, pool_scale_row)


def _out_proj_kernel(of_ref, ob_ref, z_ref, y0_ref, y1_ref, y2_ref, y3_ref, x_ref, gate_ref,
                     dng_ref, fg_ref, w_ref, out_ref):
    o = of_ref[0] + ob_ref[0]
    z = z_ref[0]
    parts = []
    for h in range(DN_HEADS):
        cs = slice(h * DN_HEAD_DIM, (h + 1) * DN_HEAD_DIM)
        oh = o[:, cs]
        ms = jnp.sum(oh * oh, axis=-1, keepdims=True) * (1.0 / DN_HEAD_DIM)
        oh = (oh * lax.rsqrt(ms + EPS)) * dng_ref[...]
        parts.append(_bf(oh * _silu(z[:, cs])))
    acc = jnp.dot(jnp.concatenate(parts, axis=1), w_ref[0:DN_WIDTH, :], preferred_element_type=F32)
    ch = y0_ref.shape[-1]
    for g, y_ref in enumerate((y0_ref, y1_ref, y2_ref, y3_ref)):
        r0 = DN_WIDTH + g * ch
        acc = acc + jnp.dot(_bf(y_ref[0]), w_ref[r0:r0 + ch, :], preferred_element_type=F32)
    xn = x_ref[0] + gate_ref[0] * acc
    d = xn.shape[-1]
    ms = jnp.sum(xn * xn, axis=-1, keepdims=True) * (1.0 / d)
    out_ref[0] = (xn * lax.rsqrt(ms + EPS)) * fg_ref[...]


def _out_proj(o_f, o_b, z_dn, y_pools, x, gate, dn_gain, final_gain, w_out_bf16, tm):
    bsz, length, d = x.shape
    ch = y_pools[0].shape[-1]
    tok = lambda w: pl.BlockSpec((1, tm, w), lambda b, j: (b, j, 0))
    const = lambda shape: pl.BlockSpec(shape, lambda b, j: tuple(0 for _ in shape))
    return pl.pallas_call(
        _out_proj_kernel,
        grid=(bsz, length // tm),
        in_specs=[tok(DN_WIDTH), tok(DN_WIDTH), tok(DN_WIDTH), tok(ch), tok(ch), tok(ch), tok(ch),
                  tok(d), pl.BlockSpec((1, 1, d), lambda b, j: (b, 0, 0)),
                  const((1, DN_HEAD_DIM)), const((1, d)), const(w_out_bf16.shape)],
        out_specs=tok(d),
        out_shape=jax.ShapeDtypeStruct((bsz, length, d), F32),
        compiler_params=_params("parallel", "arbitrary"),
        name="out_proj",
    )(o_f, o_b, z_dn, *y_pools, x, gate, dn_gain.reshape(1, DN_HEAD_DIM),
      final_gain.reshape(1, d), w_out_bf16)


def _gates_by_head_group(gates):
    bsz, length, _ = gates.shape
    n_hg = DN_HEADS // HEAD_GROUP
    g = gates[:, :, :4 * DN_HEADS].reshape(bsz, length, 2, 2, n_hg, HEAD_GROUP)
    return g.transpose(0, 4, 1, 2, 3, 5).reshape(bsz, n_hg, length, 4 * HEAD_GROUP)


def _pad_lanes(row, n=LANES):
    row = row.reshape(1, -1).astype(F32)
    return jnp.pad(row, ((0, 0), (0, n - row.shape[1])))


def _layer(x, ctx, shift_x, scale_x, gate_x, shift_c, scale_c, norm_gain, w_in, conv_w, a_log,
           dt_bias, dn_norm_gain, pool_w, pool_scale, w_out, final_gain):
    bsz, length, d = x.shape
    w3 = 3 * DN_WIDTH
    nab = 4 * DN_HEADS
    w_qkv = _bf(w_in[:, :w3])
    w_ab = _bf(jnp.pad(w_in[:, w3:w3 + nab], ((0, 0), (0, LANES - nab))))
    rest = _bf(w_in[:, w3 + nab:])
    w_zdn, w_up, w_zp = rest[:, :DN_WIDTH], rest[:, DN_WIDTH:DN_WIDTH + d], rest[:, DN_WIDTH + d:]
    cw = jnp.pad(conv_w, ((0, SUBLANES - SHORT_CONV), (0, 0)))
    alog_row, dtb_row = _pad_lanes(a_log), _pad_lanes(dt_bias)
    zero_state = jnp.zeros((bsz, DN_HEADS, DN_HEAD_DIM, DN_HEAD_DIM), F32)

    c_len = ctx.shape[1]
    qkv_c, ab_c = _in_proj(ctx, shift_c, scale_c, norm_gain, [w_qkv, w_ab], False, min(256, c_len))
    qkv_c, gates_c = _conv_gates(qkv_c, ab_c, cw, alog_row, dtb_row, min(256, c_len))
    _, _, s_f, s_b = _delta_rule(qkv_c, _gates_by_head_group(gates_c), zero_state, zero_state)

    qkv, ab, z_dn, u_pool, z_pool = _in_proj(x, shift_x, scale_x, norm_gain,
                                             [w_qkv, w_ab, w_zdn, w_up, w_zp], True, 256)
    qkv, gates = _conv_gates(qkv, ab, cw, alog_row, dtb_row, 256)
    o_f, o_b, _, _ = _delta_rule(qkv, _gates_by_head_group(gates), s_f, s_b)
    pw = _bf(pool_w)
    ps = pool_scale.reshape(1, -1)
    y_pools = [_pool_group(u_pool, z_pool, pw, ps, g) for g in range(POOL_GROUPS)]
    return _out_proj(o_f, o_b, z_dn, y_pools, x, gate_x, dn_norm_gain, final_gain, _bf(w_out), 256)


def kernel(x, c, ctx, c_ctx, w_mod, b_mod, norm_gain, w_in, conv_w, a_log, dt_bias, dn_norm_gain,
           pool_w, pool_scale, w_out, final_gain):
    bsz, length, d = x.shape
    depth = w_mod.shape[0]
    assert depth == 1, "single trunk layer"
    rows = -(-(bsz + 1) // SUBLANES) * SUBLANES
    cc = jnp.zeros((rows, d), F32).at[:bsz].set(c).at[bsz].set(c_ctx)
    mod = _mod_rows(cc, w_mod[0], b_mod[0])
    shift_x, scale_x, gate_x = (mod[:bsz, k * d:(k + 1) * d].reshape(bsz, 1, d) for k in range(3))
    shift_c, scale_c = (mod[bsz:bsz + 1, k * d:(k + 1) * d].reshape(1, 1, d) for k in range(2))
    return _layer(x, ctx, shift_x, scale_x, gate_x, shift_c, scale_c, norm_gain[0], w_in[0],
                  conv_w[0], a_log[0], dt_bias[0], dn_norm_gain[0], pool_w[0], pool_scale[0],
                  w_out[0], final_gain)
```

```python
import functools

import jax
import jax.numpy as jnp
from jax import lax
from jax.experimental import pallas as pl
from jax.experimental.pallas import tpu as pltpu

GRID_W = 64
DN_HEADS = 8
DN_HEAD_DIM = 128
DN_WIDTH = DN_HEADS * DN_HEAD_DIM
SHORT_CONV = 5
CHUNK = 64
POOL_GROUPS = 4
POOL_WINDOWS = (2, 4, 8, 16)
EPS = 1e-6

LANES = 128
SUBLANES = 8
HEAD_GROUP = 4
N_HEAD_GROUPS = DN_HEADS // HEAD_GROUP
CHUNKS_PER_STEP = 2
TOKEN_TILE = 256
COL_TILE = 512
VMEM_LIMIT = 56 * 1024 * 1024

F32 = jnp.float32
BF16 = jnp.bfloat16
HI = lax.Precision.HIGHEST


def _sigmoid(x):
    return 1.0 / (1.0 + jnp.exp(-x))


def _silu(x):
    return x * _sigmoid(x)


def _softplus(x):
    return jnp.maximum(x, 0.0) + jnp.log1p(jnp.exp(-jnp.abs(x)))


def _bf(x):
    return x.astype(BF16)


def _params(*sem):
    return pltpu.CompilerParams(dimension_semantics=sem, vmem_limit_bytes=VMEM_LIMIT)


def _mod_kernel(c_ref, w_ref, b_ref, o_ref):
    o_ref[...] = jnp.dot(_silu(c_ref[...]), w_ref[...], precision=HI,
                         preferred_element_type=F32) + b_ref[...]


def _mod_rows(cc, w_mod, b_mod):
    rows, d = cc.shape
    n = w_mod.shape[1]
    tn = 1024
    return pl.pallas_call(
        _mod_kernel,
        grid=(n // tn,),
        in_specs=[pl.BlockSpec((rows, d), lambda j: (0, 0)),
                  pl.BlockSpec((d, tn), lambda j: (0, j)),
                  pl.BlockSpec((1, tn), lambda j: (0, j))],
        out_specs=pl.BlockSpec((rows, tn), lambda j: (0, j)),
        out_shape=jax.ShapeDtypeStruct((rows, n), F32),
        compiler_params=_params("arbitrary"),
        name="mod_rows",
    )(cc, w_mod, b_mod.reshape(1, n))


def _in_proj_kernel(x_ref, xp_ref, xn_ref, shift_ref, scale_ref, gain_ref, cw_ref, alog_ref,
                    dtb_ref, wqkv_ref, wab_ref, *refs, n_extra, tm):
    w_refs = refs[:n_extra]
    qkv_ref, gates_ref = refs[n_extra:n_extra + 2]
    o_refs = refs[n_extra + 2:2 * n_extra + 2]
    hbe_ref, hb_ref = refs[2 * n_extra + 2:2 * n_extra + 4]
    ext_refs = refs[2 * n_extra + 4:]
    j = pl.program_id(1)
    nj = pl.num_programs(1)
    halo = SUBLANES
    pad = SHORT_CONV // 2
    tn = COL_TILE

    x = jnp.concatenate([xp_ref[0], x_ref[0], xn_ref[0]], axis=0)
    d = x.shape[-1]
    ms = jnp.sum(x * x, axis=-1, keepdims=True) * (1.0 / d)
    hn = (x * lax.rsqrt(ms + EPS)) * gain_ref[...]
    hn = hn * (1.0 + scale_ref[0]) + shift_ref[0]
    hbe_ref[...] = _bf(hn)
    hb_ref[...] = _bf(hn[halo:halo + tm])

    c = wqkv_ref.shape[1]
    not_first = (j > 0).astype(F32)
    not_last = (j < nj - 1).astype(F32)
    extra_chunks = [(w_ref, o_ref, n0) for w_ref, o_ref in zip(w_refs, o_refs)
                    for n0 in range(0, w_ref.shape[1], tn)]
    q_scale = float(DN_HEAD_DIM) ** -0.5
    rb = 64
    n_chunks = c // tn
    for ci in range(n_chunks):
        n0 = ci * tn
        p = jnp.dot(hbe_ref[...], wqkv_ref[:, n0:n0 + tn], preferred_element_type=F32)
        ext_ref = ext_refs[ci]
        ext_ref[0:halo, :] = p[0:halo] * not_first
        ext_ref[halo:halo + tm, :] = p[halo:halo + tm]
        ext_ref[halo + tm:, :] = p[halo + tm:] * not_last
        for w_ref, o_ref, e0 in extra_chunks[ci::n_chunks]:
            o_ref[0, :, e0:e0 + tn] = jnp.dot(hb_ref[...], w_ref[:, e0:e0 + tn],
                                              preferred_element_type=F32).astype(o_ref.dtype)
        for cb in range(n0 // LANES, (n0 + tn) // LANES):
            cs = slice(cb * LANES, (cb + 1) * LANES)
            es = slice(cb * LANES - n0, (cb + 1) * LANES - n0)
            taps = [cw_ref[t:t + 1, cs] for t in range(SHORT_CONV)]
            for r0 in range(0, tm, rb):
                acc = None
                for t in range(SHORT_CONV):
                    start = halo - pad + t + r0
                    term = ext_ref[start:start + rb, es] * taps[t]
                    acc = term if acc is None else acc + term
                y = _silu(acc)
                if cb < 2 * DN_HEADS:
                    ss = jnp.sum(y * y, axis=-1, keepdims=True)
                    y = y * lax.rsqrt(ss + EPS)
                    if cb < DN_HEADS:
                        y = y * q_scale
                qkv_ref[0, r0:r0 + rb, cs] = y

    ab_all = jnp.dot(hb_ref[...], wab_ref[...], preferred_element_type=F32)
    row = lax.broadcasted_iota(jnp.int32, (CHUNK, CHUNK), 0)
    col = lax.broadcasted_iota(jnp.int32, (CHUNK, CHUNK), 1)
    lower = (row >= col).astype(F32)
    upper = (row <= col).astype(F32)
    lane = lax.broadcasted_iota(jnp.int32, (CHUNK, LANES), 1)
    for r0 in range(0, tm, CHUNK):
        ab = ab_all[r0:r0 + CHUNK]
        g = -jnp.exp(alog_ref[...]) * _softplus(ab + dtb_ref[...])
        cum_f = jnp.dot(lower, g, precision=HI, preferred_element_type=F32)
        cum_b = jnp.dot(upper, g, precision=HI, preferred_element_type=F32)
        gates_ref[0, r0:r0 + CHUNK, :] = jnp.where(
            lane < DN_HEADS, cum_f, jnp.where(lane < 2 * DN_HEADS, cum_b, _sigmoid(ab)))


def _in_proj(h, shift, scale, gain, conv_w, alog_row, dtb_row, w_qkv, w_ab, extra, per_batch):
    bsz, length, d = h.shape
    tm = min(TOKEN_TILE, length)
    halo = SUBLANES
    per = tm // halo
    nblk8 = length // halo
    c = w_qkv.shape[1]
    mod_map = (lambda b, j: (b, 0, 0)) if per_batch else (lambda b, j: (0, 0, 0))
    const = lambda shape: pl.BlockSpec(shape, lambda b, j: tuple(0 for _ in shape))
    in_specs = [pl.BlockSpec((1, tm, d), lambda b, j: (b, j, 0)),
                pl.BlockSpec((1, halo, d), lambda b, j: (b, jnp.maximum(j * per - 1, 0), 0)),
                pl.BlockSpec((1, halo, d), lambda b, j: (b, jnp.minimum((j + 1) * per, nblk8 - 1), 0)),
                pl.BlockSpec((1, 1, d), mod_map),
                pl.BlockSpec((1, 1, d), mod_map),
                const((1, d)), const(conv_w.shape), const((1, LANES)), const((1, LANES)),
                const(w_qkv.shape), const(w_ab.shape)]
    in_specs += [const(w.shape) for w, _ in extra]
    tok = lambda w: pl.BlockSpec((1, tm, w), lambda b, j: (b, j, 0))
    out_specs = [tok(c), tok(LANES)] + [tok(w.shape[1]) for w, _ in extra]
    out_shape = [jax.ShapeDtypeStruct((bsz, length, c), F32),
                 jax.ShapeDtypeStruct((bsz, length, LANES), F32)]
    out_shape += [jax.ShapeDtypeStruct((bsz, length, w.shape[1]), dt) for w, dt in extra]
    return pl.pallas_call(
        functools.partial(_in_proj_kernel, n_extra=len(extra), tm=tm),
        grid=(bsz, length // tm),
        in_specs=in_specs,
        out_specs=out_specs,
        out_shape=out_shape,
        scratch_shapes=[pltpu.VMEM((tm + 2 * halo, d), BF16), pltpu.VMEM((tm, d), BF16)]
        + [pltpu.VMEM((tm + 2 * halo, COL_TILE), F32)] * (c // COL_TILE),
        compiler_params=_params("parallel", "arbitrary"),
        name="in_proj",
    )(h, h, h, shift, scale, gain.reshape(1, d), conv_w, alog_row, dtb_row, w_qkv, w_ab,
      *[w for w, _ in extra])


def _cat_index():
    shape = (CHUNK, HEAD_GROUP * CHUNK)
    row = lax.broadcasted_iota(jnp.int32, shape, 0)
    lane = lax.broadcasted_iota(jnp.int32, shape, 1)
    return row, jnp.bitwise_and(lane, CHUNK - 1), jnp.right_shift(lane, CHUNK.bit_length() - 1)


def _block_diag_dot(l_cat, r_cat, bd_mask):
    w = jnp.where(bd_mask, jnp.concatenate([r_cat] * HEAD_GROUP, axis=0), 0.0)
    return jnp.dot(_bf(l_cat), _bf(w), preferred_element_type=F32)


def _delta_prep(q_ref, k_ref, v_ref, g_ref, r0, hg, d, idx):
    c, dk = CHUNK, DN_HEAD_DIM
    lower = d == 0
    row, col, mat = idx
    incl = (row >= col) if lower else (row <= col)
    last = c - 1 if lower else 0
    gates = g_ref[0, r0:r0 + c, :]
    ks, kbs, qs, rhs, q_decs, k_tails, s_decs = [], [], [], [], [], [], []
    cum_cat = None
    for u in range(HEAD_GROUP):
        head = hg * HEAD_GROUP + u
        cs = slice(head * dk, (head + 1) * dk)
        ci = d * DN_HEADS + head
        bi = 2 * DN_HEADS + ci
        cum_col = gates[:, ci:ci + 1]
        cum = jnp.broadcast_to(cum_col, (c, dk))
        beta = jnp.broadcast_to(gates[:, bi:bi + 1], (c, dk))
        q, k, v = q_ref[0, r0:r0 + c, cs], k_ref[0, r0:r0 + c, cs], v_ref[0, r0:r0 + c, cs]
        kb = k * beta
        e_cum = jnp.exp(cum)
        cum_last = cum[last:last + 1, :]
        ks.append(k), kbs.append(kb), qs.append(q)
        rhs.append(jnp.concatenate([v * beta, kb * e_cum], axis=1))
        q_decs.append(q * e_cum)
        k_tails.append(k * jnp.exp(cum_last - cum))
        s_decs.append(jnp.exp(cum_last))
        cc = jnp.broadcast_to(cum_col, (c, HEAD_GROUP * c))
        cum_cat = cc if u == 0 else jnp.where(mat == u, cc, cum_cat)
    cum_row = jnp.sum(jnp.where(row == col, cum_cat, 0.0), axis=0, keepdims=True)
    decay = jnp.exp(jnp.where(incl, cum_cat - cum_row, -1e30))
    return dict(k=ks, kb=kbs, q=qs, rhs=rhs, q_dec=q_decs,
                k_tail_t=jnp.concatenate(k_tails, axis=0).T,
                s_dec=jnp.concatenate(s_decs, axis=1), decay=decay, lower=lower)


def _delta_gram(grp, idx):
    c = CHUNK
    row, col, mat = idx
    lhs = jnp.concatenate([x for kb, q in zip(grp["kb"], grp["q"]) for x in (kb, q)], axis=0)
    kcat = jnp.concatenate(grp["k"], axis=0)
    m = lax.dot_general(_bf(lhs), _bf(kcat), (((1,), (1,)), ((), ())), preferred_element_type=F32)
    a_cat = qk_cat = None
    for u in range(HEAD_GROUP):
        a_u, qk_u = m[2 * u * c:(2 * u + 1) * c], m[(2 * u + 1) * c:(2 * u + 2) * c]
        a_cat = a_u if u == 0 else jnp.where(mat == u, a_u, a_cat)
        qk_cat = qk_u if u == 0 else jnp.where(mat == u, qk_u, qk_cat)
    grp["a"] = jnp.where(row == col, 0.0, a_cat * grp["decay"])
    grp["qk"] = qk_cat * grp["decay"]


def _off_mask(idx, lower, s):
    row, col, _ = idx
    sh = s.bit_length() - 1
    rb, cb = jnp.right_shift(row, sh), jnp.right_shift(col, sh)
    if lower:
        return (jnp.bitwise_and(rb, 1) == 1) & (cb == rb - 1)
    return (jnp.bitwise_and(rb, 1) == 0) & (cb == rb + 1)


def _delta_recur(grp, sol, state, lane_head):
    c, dk = CHUNK, DN_HEAD_DIM
    zero = jnp.zeros((dk, dk), BF16)
    sb = _bf(state)
    ws = []
    for u in range(0, HEAD_GROUP, 2):
        lhs = jnp.concatenate(
            [jnp.concatenate([sol[t][:, dk:], grp["q_dec"][t]], axis=0) for t in (u, u + 1)], axis=1)
        w = jnp.concatenate(
            [jnp.concatenate([sb[:, u * dk:(u + 1) * dk], zero], axis=1),
             jnp.concatenate([zero, sb[:, (u + 1) * dk:(u + 2) * dk]], axis=1)], axis=0)
        ws.append(jnp.dot(_bf(lhs), w, preferred_element_type=F32))
    ws = jnp.concatenate(ws, axis=1)
    v_new = jnp.concatenate([s[:, 0:dk] for s in sol], axis=1) - ws[0:c]
    vb = _bf(v_new)
    bd_v = jnp.concatenate([jnp.where(lane_head == u, vb, jnp.zeros_like(vb))
                            for u in range(HEAD_GROUP)], axis=0)
    res = jnp.dot(_bf(jnp.concatenate([grp["qk"], grp["k_tail_t"]], axis=0)), bd_v,
                  preferred_element_type=F32)
    out = ws[c:2 * c] + res[0:c]
    return out, state * grp["s_dec"] + res[c:]


def _delta_kernel(qf_ref, kf_ref, vf_ref, gf_ref, qb_ref, kb_ref, vb_ref, gb_ref,
                  s0_ref, of_ref, ob_ref, s_ref, state_ref):
    i = pl.program_id(1)
    ni = pl.num_programs(1)
    c = CHUNK
    n_state = 2 * N_HEAD_GROUPS

    @pl.when(i == 0)
    def _():
        state_ref[...] = s0_ref[0]

    idx = _cat_index()
    n = HEAD_GROUP * c
    sh = c.bit_length() - 1
    bd_mask = (jnp.right_shift(lax.broadcasted_iota(jnp.int32, (n, n), 0), sh)
               == jnp.right_shift(lax.broadcasted_iota(jnp.int32, (n, n), 1), sh))
    lane_head = lax.broadcasted_iota(jnp.int32, (c, HEAD_GROUP * DN_HEAD_DIM), 1) // DN_HEAD_DIM
    eye = (idx[0] == idx[1]).astype(F32)

    keys, groups = [], []
    for slot in range(CHUNKS_PER_STEP):
        for d in range(2):
            for hg in range(N_HEAD_GROUPS):
                refs = (qf_ref, kf_ref, vf_ref, gf_ref) if d == 0 else (qb_ref, kb_ref, vb_ref, gb_ref)
                r0 = slot * c if d == 0 else (CHUNKS_PER_STEP - 1 - slot) * c
                keys.append((slot, d, hg, r0))
                groups.append(_delta_prep(*refs, r0, hg, d, idx))
    for grp in groups:
        _delta_gram(grp, idx)

    xs = [eye - jnp.where(_off_mask(idx, g["lower"], 1), g["a"], 0.0) for g in groups]
    s = 2
    while s < c:
        ls = [jnp.where(_off_mask(idx, g["lower"], s), g["a"], 0.0) for g in groups]
        ps = [_block_diag_dot(l, x, bd_mask) for l, x in zip(ls, xs)]
        xs = [x - _block_diag_dot(x, p, bd_mask) for x, p in zip(xs, ps)]
        s *= 2

    sols = [[jnp.dot(_bf(x[:, u * c:(u + 1) * c]), _bf(g["rhs"][u]), preferred_element_type=F32)
             for u in range(HEAD_GROUP)] for g, x in zip(groups, xs)]

    o_refs = (of_ref, ob_ref)
    states = [state_ref[t] for t in range(n_state)]
    hgw = HEAD_GROUP * DN_HEAD_DIM
    for (slot, d, hg, r0), grp, sol in zip(keys, groups, sols):
        t = d * N_HEAD_GROUPS + hg
        out, states[t] = _delta_recur(grp, sol, states[t], lane_head)
        o_refs[d][0, r0:r0 + c, hg * hgw:(hg + 1) * hgw] = out
    for t in range(n_state):
        state_ref[t] = states[t]

    @pl.when(i == ni - 1)
    def _():
        s_ref[0] = state_ref[...]


def _delta_rule(qkv, gates, s0):
    bsz, length, _ = qkv.shape
    rows = CHUNKS_PER_STEP * CHUNK
    nb = length // rows
    blk = (1, rows, DN_WIDTH)
    fwd = lambda col: pl.BlockSpec(blk, lambda b, i: (b, i, col))
    bwd = lambda col: pl.BlockSpec(blk, lambda b, i: (b, nb - 1 - i, col))
    gblk = (1, rows, LANES)
    sspec = pl.BlockSpec((1,) + s0.shape[1:], lambda b, i: (b, 0, 0, 0))
    o_shape = jax.ShapeDtypeStruct((bsz, length, DN_WIDTH), F32)
    return pl.pallas_call(
        _delta_kernel,
        grid=(bsz, nb),
        in_specs=[fwd(0), fwd(1), fwd(2), pl.BlockSpec(gblk, lambda b, i: (b, i, 0)),
                  bwd(0), bwd(1), bwd(2), pl.BlockSpec(gblk, lambda b, i: (b, nb - 1 - i, 0)),
                  sspec],
        out_specs=[pl.BlockSpec(blk, lambda b, i: (b, i, 0)),
                   pl.BlockSpec(blk, lambda b, i: (b, nb - 1 - i, 0)),
                   sspec],
        out_shape=[o_shape, o_shape, jax.ShapeDtypeStruct(s0.shape, F32)],
        scratch_shapes=[pltpu.VMEM(s0.shape[1:], F32)],
        compiler_params=_params("parallel", "arbitrary"),
        name="delta_rule",
    )(qkv, qkv, qkv, gates, qkv, qkv, qkv, gates, s0)


def _pool_kernel(u_ref, z_ref, w_ref, scale_ref, y_ref, cs_ref, d_ref, *, win, rows):
    half = win // 2
    gw = GRID_W
    ch = u_ref.shape[-1]
    cidx = lax.broadcasted_iota(jnp.int32, (gw, ch), 0)
    col_cnt = (jnp.minimum(cidx - half + win, gw) - jnp.maximum(cidx - half, 0)).astype(F32)
    zeros = jnp.zeros((gw, ch), F32)
    for r in range(half):
        cs_ref[r * gw:(r + 1) * gw, :] = zeros
    for r in range(half - 1):
        cs_ref[(half + rows + r) * gw:(half + rows + r + 1) * gw, :] = zeros

    def col_pass(r, carry):
        start = pl.multiple_of(r * gw, gw)
        tile = u_ref[0, pl.ds(start, gw), :]
        acc = None
        for dc in range(-half, half):
            if dc == 0:
                term = tile
            else:
                rolled = pltpu.roll(tile, (-dc) % gw, 0)
                ok = (cidx + dc >= 0) & (cidx + dc < gw)
                term = jnp.where(ok, rolled, 0.0)
            acc = term if acc is None else acc + term
        cs_ref[pl.ds(pl.multiple_of((r + half) * gw, gw), gw), :] = acc
        return carry

    lax.fori_loop(0, rows, col_pass, 0)

    def row_pass(r, carry):
        acc = None
        for k in range(win):
            term = cs_ref[pl.ds(pl.multiple_of((r + k) * gw, gw), gw), :]
            acc = term if acc is None else acc + term
        row_cnt = (jnp.minimum(r - half + win, rows) - jnp.maximum(r - half, 0)).astype(F32)
        start = pl.multiple_of(r * gw, gw)
        d_ref[pl.ds(start, gw), :] = acc / (col_cnt * row_cnt) - u_ref[0, pl.ds(start, gw), :]
        return carry

    lax.fori_loop(0, rows, row_pass, 0)

    tm = min(512, rows * gw)
    for m0 in range(0, rows * gw, tm):
        y = jnp.dot(_bf(d_ref[m0:m0 + tm, :]), w_ref[0], preferred_element_type=F32)
        y = y * scale_ref[...] * _silu(z_ref[0, m0:m0 + tm, :])
        y_ref[0, m0:m0 + tm, :] = y.astype(y_ref.dtype)


def _pool_group(u_pool, z_pool, pool_w_bf16, pool_scale_row, g):
    bsz, length, width = u_pool.shape
    ch = width // POOL_GROUPS
    rows = length // GRID_W
    win = POOL_WINDOWS[g]
    return pl.pallas_call(
        functools.partial(_pool_kernel, win=win, rows=rows),
        grid=(bsz,),
        in_specs=[pl.BlockSpec((1, length, ch), lambda b: (b, 0, g)),
                  pl.BlockSpec((1, length, ch), lambda b: (b, 0, g)),
                  pl.BlockSpec((1, ch, ch), lambda b: (g, 0, 0)),
                  pl.BlockSpec((1, ch), lambda b: (0, g))],
        out_specs=pl.BlockSpec((1, length, ch), lambda b: (b, 0, 0)),
        out_shape=jax.ShapeDtypeStruct((bsz, length, ch), BF16),
        scratch_shapes=[pltpu.VMEM(((rows + win) * GRID_W, ch), F32),
                        pltpu.VMEM((length, ch), F32)],
        compiler_params=_params("parallel"),
        name=f"pool_w{win}",
    )(u_pool, z_pool, pool_w_bf16, pool_scale_row)


def _out_proj_kernel(of_ref, ob_ref, z_ref, y0_ref, y1_ref, y2_ref, y3_ref, x_ref, gate_ref,
                     dng_ref, fg_ref, w_ref, out_ref):
    o = of_ref[0] + ob_ref[0]
    z = z_ref[0]
    parts = []
    for h in range(DN_HEADS):
        cs = slice(h * DN_HEAD_DIM, (h + 1) * DN_HEAD_DIM)
        oh = o[:, cs]
        ms = jnp.sum(oh * oh, axis=-1, keepdims=True) * (1.0 / DN_HEAD_DIM)
        oh = (oh * lax.rsqrt(ms + EPS)) * dng_ref[...]
        parts.append(_bf(oh * _silu(z[:, cs])))
    acc = jnp.dot(jnp.concatenate(parts, axis=1), w_ref[0:DN_WIDTH, :], preferred_element_type=F32)
    ch = y0_ref.shape[-1]
    for g, y_ref in enumerate((y0_ref, y1_ref, y2_ref, y3_ref)):
        r0 = DN_WIDTH + g * ch
        acc = acc + jnp.dot(y_ref[0], w_ref[r0:r0 + ch, :], preferred_element_type=F32)
    xn = x_ref[0] + gate_ref[0] * acc
    d = xn.shape[-1]
    ms = jnp.sum(xn * xn, axis=-1, keepdims=True) * (1.0 / d)
    out_ref[0] = (xn * lax.rsqrt(ms + EPS)) * fg_ref[...]


def _out_proj(o_f, o_b, z_dn, y_pools, x, gate, dn_gain, final_gain, w_out_bf16):
    bsz, length, d = x.shape
    tm = min(TOKEN_TILE, length)
    ch = y_pools[0].shape[-1]
    tok = lambda w: pl.BlockSpec((1, tm, w), lambda b, j: (b, j, 0))
    const = lambda shape: pl.BlockSpec(shape, lambda b, j: tuple(0 for _ in shape))
    return pl.pallas_call(
        _out_proj_kernel,
        grid=(bsz, length // tm),
        in_specs=[tok(DN_WIDTH), tok(DN_WIDTH), tok(DN_WIDTH), tok(ch), tok(ch), tok(ch), tok(ch),
                  tok(d), pl.BlockSpec((1, 1, d), lambda b, j: (b, 0, 0)),
                  const((1, DN_HEAD_DIM)), const((1, d)), const(w_out_bf16.shape)],
        out_specs=tok(d),
        out_shape=jax.ShapeDtypeStruct((bsz, length, d), F32),
        compiler_params=_params("parallel", "arbitrary"),
        name="out_proj",
    )(o_f, o_b, z_dn, *y_pools, x, gate, dn_gain.reshape(1, DN_HEAD_DIM),
      final_gain.reshape(1, d), w_out_bf16)


def _pad_lanes(row, n=LANES):
    row = row.reshape(1, -1).astype(F32)
    return jnp.pad(row, ((0, 0), (0, n - row.shape[1])))


def _layer(x, ctx, shift_x, scale_x, gate_x, shift_c, scale_c, norm_gain, w_in, conv_w, a_log,
           dt_bias, dn_norm_gain, pool_w, pool_scale, w_out, final_gain):
    bsz, length, d = x.shape
    w3 = 3 * DN_WIDTH
    nab = 4 * DN_HEADS
    w_qkv = _bf(w_in[:, :w3])
    w_ab = _bf(jnp.pad(w_in[:, w3:w3 + nab], ((0, 0), (0, LANES - nab))))
    rest = _bf(w_in[:, w3 + nab:])
    w_zdn, w_up, w_zp = rest[:, :DN_WIDTH], rest[:, DN_WIDTH:DN_WIDTH + d], rest[:, DN_WIDTH + d:]
    cw = jnp.pad(conv_w, ((0, SUBLANES - SHORT_CONV), (0, 0)))
    alog_row, dtb_row = _pad_lanes(a_log), _pad_lanes(dt_bias)
    zero_state = jnp.zeros((bsz, 2 * N_HEAD_GROUPS, DN_HEAD_DIM, HEAD_GROUP * DN_HEAD_DIM), F32)

    qkv_c, gates_c = _in_proj(ctx, shift_c, scale_c, norm_gain, cw, alog_row, dtb_row,
                              w_qkv, w_ab, [], False)
    _, _, s_ctx = _delta_rule(qkv_c, gates_c, zero_state)

    qkv, gates, z_dn, u_pool, z_pool = _in_proj(
        x, shift_x, scale_x, norm_gain, cw, alog_row, dtb_row, w_qkv, w_ab,
        [(w_zdn, F32), (w_up, F32), (w_zp, F32)], True)
    o_f, o_b, _ = _delta_rule(qkv, gates, s_ctx)
    pw = _bf(pool_w)
    ps = pool_scale.reshape(1, -1)
    y_pools = [_pool_group(u_pool, z_pool, pw, ps, g) for g in range(POOL_GROUPS)]
    return _out_proj(o_f, o_b, z_dn, y_pools, x, gate_x, dn_norm_gain, final_gain, _bf(w_out))


def kernel(x, c, ctx, c_ctx, w_mod, b_mod, norm_gain, w_in, conv_w, a_log, dt_bias, dn_norm_gain,
           pool_w, pool_scale, w_out, final_gain):
    bsz, length, d = x.shape
    depth = w_mod.shape[0]
    assert depth == 1, "single trunk layer"
    rows = -(-(bsz + 1) // SUBLANES) * SUBLANES
    cc = jnp.zeros((rows, d), F32).at[:bsz].set(c).at[bsz].set(c_ctx)
    mod = _mod_rows(cc, w_mod[0], b_mod[0])
    shift_x, scale_x, gate_x = (mod[:bsz, k * d:(k + 1) * d].reshape(bsz, 1, d) for k in range(3))
    shift_c, scale_c = (mod[bsz:bsz + 1, k * d:(k + 1) * d].reshape(1, 1, d) for k in range(2))
    return _layer(x, ctx, shift_x, scale_x, gate_x, shift_c, scale_c, norm_gain[0], w_in[0],
                  conv_w[0], a_log[0], dt_bias[0], dn_norm_gain[0], pool_w[0], pool_scale[0],
                  w_out[0], final_gain)
```

```python
import functools

import jax
import jax.numpy as jnp
from jax import lax
from jax.experimental import pallas as pl
from jax.experimental.pallas import tpu as pltpu

GRID_W = 64
DN_HEADS = 8
DN_HEAD_DIM = 128
DN_WIDTH = DN_HEADS * DN_HEAD_DIM
SHORT_CONV = 5
CHUNK = 64
POOL_GROUPS = 4
POOL_WINDOWS = (2, 4, 8, 16)
EPS = 1e-6

LANES = 128
SUBLANES = 8
HEAD_GROUP = 4
N_HEAD_GROUPS = DN_HEADS // HEAD_GROUP
CHUNKS_PER_STEP = 4
TOKEN_TILE = 256
COL_TILE = 512
VMEM_LIMIT = 56 * 1024 * 1024

F32 = jnp.float32
BF16 = jnp.bfloat16
HI = lax.Precision.HIGHEST


def _sigmoid(x):
    return 1.0 / (1.0 + jnp.exp(-x))


def _silu(x):
    return x * _sigmoid(x)


def _softplus(x):
    return jnp.maximum(x, 0.0) + jnp.log1p(jnp.exp(-jnp.abs(x)))


def _bf(x):
    return x.astype(BF16)


def _params(*sem):
    return pltpu.CompilerParams(dimension_semantics=sem, vmem_limit_bytes=VMEM_LIMIT)


def _mod_kernel(c_ref, w_ref, b_ref, o_ref):
    o_ref[...] = jnp.dot(_silu(c_ref[...]), w_ref[...], precision=HI,
                         preferred_element_type=F32) + b_ref[...]


def _mod_rows(cc, w_mod, b_mod):
    rows, d = cc.shape
    n = w_mod.shape[1]
    tn = 1024
    return pl.pallas_call(
        _mod_kernel,
        grid=(n // tn,),
        in_specs=[pl.BlockSpec((rows, d), lambda j: (0, 0)),
                  pl.BlockSpec((d, tn), lambda j: (0, j)),
                  pl.BlockSpec((1, tn), lambda j: (0, j))],
        out_specs=pl.BlockSpec((rows, tn), lambda j: (0, j)),
        out_shape=jax.ShapeDtypeStruct((rows, n), F32),
        compiler_params=_params("arbitrary"),
        name="mod_rows",
    )(cc, w_mod, b_mod.reshape(1, n))


def _in_proj_kernel(x_ref, xp_ref, xn_ref, shift_ref, scale_ref, gain_ref, cw_ref, alog_ref,
                    dtb_ref, wqkv_ref, wab_ref, *refs, n_extra, tm):
    w_refs = refs[:n_extra]
    qkv_ref, gates_ref = refs[n_extra:n_extra + 2]
    o_refs = refs[n_extra + 2:2 * n_extra + 2]
    hbe_ref, hb_ref = refs[2 * n_extra + 2:2 * n_extra + 4]
    ext_refs = refs[2 * n_extra + 4:]
    j = pl.program_id(1)
    nj = pl.num_programs(1)
    halo = SUBLANES
    pad = SHORT_CONV // 2
    tn = COL_TILE

    x = jnp.concatenate([xp_ref[0], x_ref[0], xn_ref[0]], axis=0)
    d = x.shape[-1]
    ms = jnp.sum(x * x, axis=-1, keepdims=True) * (1.0 / d)
    hn = (x * lax.rsqrt(ms + EPS)) * gain_ref[...]
    hn = hn * (1.0 + scale_ref[0]) + shift_ref[0]
    hbe_ref[...] = _bf(hn)
    hb_ref[...] = _bf(hn[halo:halo + tm])

    c = wqkv_ref.shape[1]
    not_first = (j > 0).astype(F32)
    not_last = (j < nj - 1).astype(F32)
    extra_chunks = [(w_ref, o_ref, n0) for w_ref, o_ref in zip(w_refs, o_refs)
                    for n0 in range(0, w_ref.shape[1], tn)]
    q_scale = float(DN_HEAD_DIM) ** -0.5
    rb = 64
    n_chunks = c // tn
    for ci in range(n_chunks):
        n0 = ci * tn
        p = jnp.dot(hbe_ref[...], wqkv_ref[:, n0:n0 + tn], preferred_element_type=F32)
        ext_ref = ext_refs[ci]
        ext_ref[0:halo, :] = p[0:halo] * not_first
        ext_ref[halo:halo + tm, :] = p[halo:halo + tm]
        ext_ref[halo + tm:, :] = p[halo + tm:] * not_last
        for w_ref, o_ref, e0 in extra_chunks[ci::n_chunks]:
            o_ref[0, :, e0:e0 + tn] = jnp.dot(hb_ref[...], w_ref[:, e0:e0 + tn],
                                              preferred_element_type=F32).astype(o_ref.dtype)
        for cb in range(n0 // LANES, (n0 + tn) // LANES):
            cs = slice(cb * LANES, (cb + 1) * LANES)
            es = slice(cb * LANES - n0, (cb + 1) * LANES - n0)
            taps = [cw_ref[t:t + 1, cs] for t in range(SHORT_CONV)]
            for r0 in range(0, tm, rb):
                acc = None
                for t in range(SHORT_CONV):
                    start = halo - pad + t + r0
                    term = ext_ref[start:start + rb, es] * taps[t]
                    acc = term if acc is None else acc + term
                y = _silu(acc)
                if cb < 2 * DN_HEADS:
                    inv = lax.rsqrt(jnp.sum(y * y, axis=-1, keepdims=True) + EPS)
                    y = y * (inv * q_scale if cb < DN_HEADS else inv)
                qkv_ref[0, r0:r0 + rb, cs] = y

    ab_all = jnp.dot(hb_ref[...], wab_ref[...], preferred_element_type=F32)
    row = lax.broadcasted_iota(jnp.int32, (CHUNK, CHUNK), 0)
    col = lax.broadcasted_iota(jnp.int32, (CHUNK, CHUNK), 1)
    lower = (row >= col).astype(F32)
    upper = (row <= col).astype(F32)
    lane = lax.broadcasted_iota(jnp.int32, (CHUNK, LANES), 1)
    for r0 in range(0, tm, CHUNK):
        ab = ab_all[r0:r0 + CHUNK]
        g = -jnp.exp(alog_ref[...]) * _softplus(ab + dtb_ref[...])
        cum_f = jnp.dot(lower, g, precision=HI, preferred_element_type=F32)
        cum_b = jnp.dot(upper, g, precision=HI, preferred_element_type=F32)
        gates_ref[0, r0:r0 + CHUNK, :] = jnp.where(
            lane < DN_HEADS, cum_f, jnp.where(lane < 2 * DN_HEADS, cum_b, _sigmoid(ab)))


def _in_proj(h, shift, scale, gain, conv_w, alog_row, dtb_row, w_qkv, w_ab, extra, per_batch):
    bsz, length, d = h.shape
    tm = min(TOKEN_TILE, length)
    halo = SUBLANES
    per = tm // halo
    nblk8 = length // halo
    c = w_qkv.shape[1]
    mod_map = (lambda b, j: (b, 0, 0)) if per_batch else (lambda b, j: (0, 0, 0))
    const = lambda shape: pl.BlockSpec(shape, lambda b, j: tuple(0 for _ in shape))
    in_specs = [pl.BlockSpec((1, tm, d), lambda b, j: (b, j, 0)),
                pl.BlockSpec((1, halo, d), lambda b, j: (b, jnp.maximum(j * per - 1, 0), 0)),
                pl.BlockSpec((1, halo, d), lambda b, j: (b, jnp.minimum((j + 1) * per, nblk8 - 1), 0)),
                pl.BlockSpec((1, 1, d), mod_map),
                pl.BlockSpec((1, 1, d), mod_map),
                const((1, d)), const(conv_w.shape), const((1, LANES)), const((1, LANES)),
                const(w_qkv.shape), const(w_ab.shape)]
    in_specs += [const(w.shape) for w, _ in extra]
    tok = lambda w: pl.BlockSpec((1, tm, w), lambda b, j: (b, j, 0))
    out_specs = [tok(c), tok(LANES)] + [tok(w.shape[1]) for w, _ in extra]
    out_shape = [jax.ShapeDtypeStruct((bsz, length, c), F32),
                 jax.ShapeDtypeStruct((bsz, length, LANES), F32)]
    out_shape += [jax.ShapeDtypeStruct((bsz, length, w.shape[1]), dt) for w, dt in extra]
    return pl.pallas_call(
        functools.partial(_in_proj_kernel, n_extra=len(extra), tm=tm),
        grid=(bsz, length // tm),
        in_specs=in_specs,
        out_specs=out_specs,
        out_shape=out_shape,
        scratch_shapes=[pltpu.VMEM((tm + 2 * halo, d), BF16), pltpu.VMEM((tm, d), BF16)]
        + [pltpu.VMEM((tm + 2 * halo, COL_TILE), F32)] * (c // COL_TILE),
        compiler_params=_params("parallel", "arbitrary"),
        name="in_proj",
    )(h, h, h, shift, scale, gain.reshape(1, d), conv_w, alog_row, dtb_row, w_qkv, w_ab,
      *[w for w, _ in extra])


def _cat_index():
    shape = (CHUNK, HEAD_GROUP * CHUNK)
    row = lax.broadcasted_iota(jnp.int32, shape, 0)
    lane = lax.broadcasted_iota(jnp.int32, shape, 1)
    return row, jnp.bitwise_and(lane, CHUNK - 1), jnp.right_shift(lane, CHUNK.bit_length() - 1)


def _block_diag_dot(l_cat, r_cat, bd_mask):
    w = jnp.where(bd_mask, jnp.concatenate([r_cat] * HEAD_GROUP, axis=0), 0.0)
    return jnp.dot(_bf(l_cat), _bf(w), preferred_element_type=F32)


def _delta_prep(q_ref, k_ref, v_ref, g_ref, r0, hg, d, idx):
    c, dk = CHUNK, DN_HEAD_DIM
    lower = d == 0
    row, col, mat = idx
    incl = (row >= col) if lower else (row <= col)
    last = c - 1 if lower else 0
    gates = g_ref[0, r0:r0 + c, :]
    ks, kbs, qs, rhs, q_decs, k_tails, s_decs = [], [], [], [], [], [], []
    cum_cat = None
    for u in range(HEAD_GROUP):
        head = hg * HEAD_GROUP + u
        cs = slice(head * dk, (head + 1) * dk)
        ci = d * DN_HEADS + head
        bi = 2 * DN_HEADS + ci
        cum_col = gates[:, ci:ci + 1]
        cum = jnp.broadcast_to(cum_col, (c, dk))
        beta = jnp.broadcast_to(gates[:, bi:bi + 1], (c, dk))
        q, k, v = q_ref[0, r0:r0 + c, cs], k_ref[0, r0:r0 + c, cs], v_ref[0, r0:r0 + c, cs]
        kb = k * beta
        e_cum = jnp.exp(cum)
        cum_last = cum[last:last + 1, :]
        ks.append(k), kbs.append(kb), qs.append(q)
        rhs.append(jnp.concatenate([v * beta, kb * e_cum], axis=1))
        q_decs.append(q * e_cum)
        k_tails.append(k * jnp.exp(cum_last - cum))
        s_decs.append(jnp.exp(cum_last))
        cc = jnp.broadcast_to(cum_col, (c, HEAD_GROUP * c))
        cum_cat = cc if u == 0 else jnp.where(mat == u, cc, cum_cat)
    cum_row = jnp.sum(jnp.where(row == col, cum_cat, 0.0), axis=0, keepdims=True)
    decay = jnp.exp(jnp.where(incl, cum_cat - cum_row, -1e30))
    return dict(k=ks, kb=kbs, q=qs, rhs=rhs, q_dec=q_decs,
                k_tail_t=jnp.concatenate(k_tails, axis=0).T,
                s_dec=jnp.concatenate(s_decs, axis=1), decay=decay, lower=lower)


def _delta_gram(grp, idx):
    c = CHUNK
    row, col, mat = idx
    lhs = jnp.concatenate([x for kb, q in zip(grp["kb"], grp["q"]) for x in (kb, q)], axis=0)
    kcat = jnp.concatenate(grp["k"], axis=0)
    m = lax.dot_general(_bf(lhs), _bf(kcat), (((1,), (1,)), ((), ())), preferred_element_type=F32)
    a_cat = qk_cat = None
    for u in range(HEAD_GROUP):
        a_u, qk_u = m[2 * u * c:(2 * u + 1) * c], m[(2 * u + 1) * c:(2 * u + 2) * c]
        a_cat = a_u if u == 0 else jnp.where(mat == u, a_u, a_cat)
        qk_cat = qk_u if u == 0 else jnp.where(mat == u, qk_u, qk_cat)
    grp["a"] = jnp.where(row == col, 0.0, a_cat * grp["decay"])
    grp["qk"] = qk_cat * grp["decay"]


def _off_mask(idx, lower, s):
    row, col, _ = idx
    sh = s.bit_length() - 1
    rb, cb = jnp.right_shift(row, sh), jnp.right_shift(col, sh)
    if lower:
        return (jnp.bitwise_and(rb, 1) == 1) & (cb == rb - 1)
    return (jnp.bitwise_and(rb, 1) == 0) & (cb == rb + 1)


def _recur_ws(grp, sol, state):
    c, dk = CHUNK, DN_HEAD_DIM
    zero = jnp.zeros((dk, dk), BF16)
    sb = _bf(state)
    ws = []
    for u in range(0, HEAD_GROUP, 2):
        lhs = jnp.concatenate(
            [jnp.concatenate([sol[t][:, dk:], grp["q_dec"][t]], axis=0) for t in (u, u + 1)], axis=1)
        w = jnp.concatenate(
            [jnp.concatenate([sb[:, u * dk:(u + 1) * dk], zero], axis=1),
             jnp.concatenate([zero, sb[:, (u + 1) * dk:(u + 2) * dk]], axis=1)], axis=0)
        ws.append(jnp.dot(_bf(lhs), w, preferred_element_type=F32))
    ws = jnp.concatenate(ws, axis=1)
    v_new = jnp.concatenate([s[:, 0:dk] for s in sol], axis=1) - ws[0:c]
    return ws, v_new


def _recur_out(grp, ws, v_new, state, lane_head):
    c = CHUNK
    vb = _bf(v_new)
    bd_v = jnp.concatenate([jnp.where(lane_head == u, vb, jnp.zeros_like(vb))
                            for u in range(HEAD_GROUP)], axis=0)
    res = jnp.dot(_bf(jnp.concatenate([grp["qk"], grp["k_tail_t"]], axis=0)), bd_v,
                  preferred_element_type=F32)
    return ws[c:2 * c] + res[0:c], state * grp["s_dec"] + res[c:]


def _delta_kernel(qf_ref, kf_ref, vf_ref, gf_ref, qb_ref, kb_ref, vb_ref, gb_ref,
                  s0_ref, of_ref, ob_ref, s_ref, state_ref):
    i = pl.program_id(1)
    ni = pl.num_programs(1)
    c = CHUNK
    n_state = 2 * N_HEAD_GROUPS

    @pl.when(i == 0)
    def _():
        state_ref[...] = s0_ref[0]

    idx = _cat_index()
    n = HEAD_GROUP * c
    sh = c.bit_length() - 1
    bd_mask = (jnp.right_shift(lax.broadcasted_iota(jnp.int32, (n, n), 0), sh)
               == jnp.right_shift(lax.broadcasted_iota(jnp.int32, (n, n), 1), sh))
    lane_head = lax.broadcasted_iota(jnp.int32, (c, HEAD_GROUP * DN_HEAD_DIM), 1) // DN_HEAD_DIM
    eye = (idx[0] == idx[1]).astype(F32)

    keys, groups = [], []
    for slot in range(CHUNKS_PER_STEP):
        for d in range(2):
            for hg in range(N_HEAD_GROUPS):
                refs = (qf_ref, kf_ref, vf_ref, gf_ref) if d == 0 else (qb_ref, kb_ref, vb_ref, gb_ref)
                r0 = slot * c if d == 0 else (CHUNKS_PER_STEP - 1 - slot) * c
                keys.append((slot, d, hg, r0))
                groups.append(_delta_prep(*refs, r0, hg, d, idx))
    for grp in groups:
        _delta_gram(grp, idx)

    xs = [eye - jnp.where(_off_mask(idx, g["lower"], 1), g["a"], 0.0) for g in groups]
    s = 2
    while s < c:
        ls = [jnp.where(_off_mask(idx, g["lower"], s), g["a"], 0.0) for g in groups]
        ps = [_block_diag_dot(l, x, bd_mask) for l, x in zip(ls, xs)]
        xs = [x - _block_diag_dot(x, p, bd_mask) for x, p in zip(xs, ps)]
        s *= 2

    sols = [[jnp.dot(_bf(x[:, u * c:(u + 1) * c]), _bf(g["rhs"][u]), preferred_element_type=F32)
             for u in range(HEAD_GROUP)] for g, x in zip(groups, xs)]

    o_refs = (of_ref, ob_ref)
    states = [state_ref[t] for t in range(n_state)]
    hgw = HEAD_GROUP * DN_HEAD_DIM
    per_slot = 2 * N_HEAD_GROUPS
    for slot in range(CHUNKS_PER_STEP):
        sel = range(slot * per_slot, (slot + 1) * per_slot)
        tids = [keys[n][1] * N_HEAD_GROUPS + keys[n][2] for n in sel]
        wv = [_recur_ws(groups[n], sols[n], states[t]) for n, t in zip(sel, tids)]
        for n, t, (ws, v_new) in zip(sel, tids, wv):
            _, d, hg, r0 = keys[n]
            out, states[t] = _recur_out(groups[n], ws, v_new, states[t], lane_head)
            o_refs[d][0, r0:r0 + c, hg * hgw:(hg + 1) * hgw] = out.astype(o_refs[d].dtype)
    for t in range(n_state):
        state_ref[t] = states[t]

    @pl.when(i == ni - 1)
    def _():
        s_ref[0] = state_ref[...]


def _delta_rule(qkv, gates, s0):
    bsz, length, _ = qkv.shape
    rows = CHUNKS_PER_STEP * CHUNK
    nb = length // rows
    assert length % rows == 0, "sequence length must be a multiple of the delta-rule step"
    blk = (1, rows, DN_WIDTH)
    fwd = lambda col: pl.BlockSpec(blk, lambda b, i: (b, i, col))
    bwd = lambda col: pl.BlockSpec(blk, lambda b, i: (b, nb - 1 - i, col))
    gblk = (1, rows, LANES)
    sspec = pl.BlockSpec((1,) + s0.shape[1:], lambda b, i: (b, 0, 0, 0))
    o_shape = jax.ShapeDtypeStruct((bsz, length, DN_WIDTH), BF16)
    return pl.pallas_call(
        _delta_kernel,
        grid=(bsz, nb),
        in_specs=[fwd(0), fwd(1), fwd(2), pl.BlockSpec(gblk, lambda b, i: (b, i, 0)),
                  bwd(0), bwd(1), bwd(2), pl.BlockSpec(gblk, lambda b, i: (b, nb - 1 - i, 0)),
                  sspec],
        out_specs=[pl.BlockSpec(blk, lambda b, i: (b, i, 0)),
                   pl.BlockSpec(blk, lambda b, i: (b, nb - 1 - i, 0)),
                   sspec],
        out_shape=[o_shape, o_shape, jax.ShapeDtypeStruct(s0.shape, F32)],
        scratch_shapes=[pltpu.VMEM(s0.shape[1:], F32)],
        compiler_params=_params("parallel", "arbitrary"),
        name="delta_rule",
    )(qkv, qkv, qkv, gates, qkv, qkv, qkv, gates, s0)


def _pool_kernel(u_ref, z_ref, w_ref, scale_ref, y_ref, cs_ref, d_ref, *, win, rows):
    half = win // 2
    gw = GRID_W
    ch = u_ref.shape[-1]
    cidx = lax.broadcasted_iota(jnp.int32, (gw, ch), 0)
    col_cnt = (jnp.minimum(cidx - half + win, gw) - jnp.maximum(cidx - half, 0)).astype(F32)
    zeros = jnp.zeros((gw, ch), F32)
    for r in range(half):
        cs_ref[r * gw:(r + 1) * gw, :] = zeros
    for r in range(half - 1):
        cs_ref[(half + rows + r) * gw:(half + rows + r + 1) * gw, :] = zeros

    def col_pass(r, carry):
        start = pl.multiple_of(r * gw, gw)
        tile = u_ref[0, pl.ds(start, gw), :]
        acc = None
        for dc in range(-half, half):
            if dc == 0:
                term = tile
            else:
                rolled = pltpu.roll(tile, (-dc) % gw, 0)
                ok = (cidx + dc >= 0) & (cidx + dc < gw)
                term = jnp.where(ok, rolled, 0.0)
            acc = term if acc is None else acc + term
        cs_ref[pl.ds(pl.multiple_of((r + half) * gw, gw), gw), :] = acc
        return carry

    lax.fori_loop(0, rows, col_pass, 0)

    def row_pass(r, carry):
        acc = None
        for k in range(win):
            term = cs_ref[pl.ds(pl.multiple_of((r + k) * gw, gw), gw), :]
            acc = term if acc is None else acc + term
        row_cnt = (jnp.minimum(r - half + win, rows) - jnp.maximum(r - half, 0)).astype(F32)
        start = pl.multiple_of(r * gw, gw)
        d_ref[pl.ds(start, gw), :] = acc / (col_cnt * row_cnt) - u_ref[0, pl.ds(start, gw), :]
        return carry

    lax.fori_loop(0, rows, row_pass, 0)

    tm = min(512, rows * gw)
    for m0 in range(0, rows * gw, tm):
        y = jnp.dot(_bf(d_ref[m0:m0 + tm, :]), w_ref[0], preferred_element_type=F32)
        y = y * scale_ref[...] * _silu(z_ref[0, m0:m0 + tm, :].astype(F32))
        y_ref[0, m0:m0 + tm, :] = y.astype(y_ref.dtype)


def _pool_group(u_pool, z_pool, pool_w_bf16, pool_scale_row, g):
    bsz, length, width = u_pool.shape
    ch = width // POOL_GROUPS
    rows = length // GRID_W
    win = POOL_WINDOWS[g]
    return pl.pallas_call(
        functools.partial(_pool_kernel, win=win, rows=rows),
        grid=(bsz,),
        in_specs=[pl.BlockSpec((1, length, ch), lambda b: (b, 0, g)),
                  pl.BlockSpec((1, length, ch), lambda b: (b, 0, g)),
                  pl.BlockSpec((1, ch, ch), lambda b: (g, 0, 0)),
                  pl.BlockSpec((1, ch), lambda b: (0, g))],
        out_specs=pl.BlockSpec((1, length, ch), lambda b: (b, 0, 0)),
        out_shape=jax.ShapeDtypeStruct((bsz, length, ch), BF16),
        scratch_shapes=[pltpu.VMEM(((rows + win) * GRID_W, ch), F32),
                        pltpu.VMEM((length, ch), F32)],
        compiler_params=_params("parallel"),
        name=f"pool_w{win}",
    )(u_pool, z_pool, pool_w_bf16, pool_scale_row)


def _out_proj_kernel(of_ref, ob_ref, z_ref, y0_ref, y1_ref, y2_ref, y3_ref, x_ref, gate_ref,
                     dng_ref, fg_ref, w_ref, out_ref):
    o = of_ref[0].astype(F32) + ob_ref[0].astype(F32)
    z = z_ref[0].astype(F32)
    parts = []
    for h in range(DN_HEADS):
        cs = slice(h * DN_HEAD_DIM, (h + 1) * DN_HEAD_DIM)
        oh = o[:, cs]
        ms = jnp.sum(oh * oh, axis=-1, keepdims=True) * (1.0 / DN_HEAD_DIM)
        oh = (oh * lax.rsqrt(ms + EPS)) * dng_ref[...]
        parts.append(_bf(oh * _silu(z[:, cs])))
    acc = jnp.dot(jnp.concatenate(parts, axis=1), w_ref[0:DN_WIDTH, :], preferred_element_type=F32)
    ch = y0_ref.shape[-1]
    for g, y_ref in enumerate((y0_ref, y1_ref, y2_ref, y3_ref)):
        r0 = DN_WIDTH + g * ch
        acc = acc + jnp.dot(y_ref[0], w_ref[r0:r0 + ch, :], preferred_element_type=F32)
    xn = x_ref[0] + gate_ref[0] * acc
    d = xn.shape[-1]
    ms = jnp.sum(xn * xn, axis=-1, keepdims=True) * (1.0 / d)
    out_ref[0] = (xn * lax.rsqrt(ms + EPS)) * fg_ref[...]


def _out_proj(o_f, o_b, z_dn, y_pools, x, gate, dn_gain, final_gain, w_out_bf16):
    bsz, length, d = x.shape
    tm = min(TOKEN_TILE, length)
    ch = y_pools[0].shape[-1]
    tok = lambda w: pl.BlockSpec((1, tm, w), lambda b, j: (b, j, 0))
    const = lambda shape: pl.BlockSpec(shape, lambda b, j: tuple(0 for _ in shape))
    return pl.pallas_call(
        _out_proj_kernel,
        grid=(bsz, length // tm),
        in_specs=[tok(DN_WIDTH), tok(DN_WIDTH), tok(DN_WIDTH), tok(ch), tok(ch), tok(ch), tok(ch),
                  tok(d), pl.BlockSpec((1, 1, d), lambda b, j: (b, 0, 0)),
                  const((1, DN_HEAD_DIM)), const((1, d)), const(w_out_bf16.shape)],
        out_specs=tok(d),
        out_shape=jax.ShapeDtypeStruct((bsz, length, d), F32),
        compiler_params=_params("parallel", "arbitrary"),
        name="out_proj",
    )(o_f, o_b, z_dn, *y_pools, x, gate, dn_gain.reshape(1, DN_HEAD_DIM),
      final_gain.reshape(1, d), w_out_bf16)


def _pad_lanes(row, n=LANES):
    row = row.reshape(1, -1).astype(F32)
    return jnp.pad(row, ((0, 0), (0, n - row.shape[1])))


def _layer(x, ctx, shift_x, scale_x, gate_x, shift_c, scale_c, norm_gain, w_in, conv_w, a_log,
           dt_bias, dn_norm_gain, pool_w, pool_scale, w_out, final_gain):
    bsz, length, d = x.shape
    w3 = 3 * DN_WIDTH
    nab = 4 * DN_HEADS
    w_qkv = _bf(w_in[:, :w3])
    w_ab = _bf(jnp.pad(w_in[:, w3:w3 + nab], ((0, 0), (0, LANES - nab))))
    rest = _bf(w_in[:, w3 + nab:])
    w_zdn, w_up, w_zp = rest[:, :DN_WIDTH], rest[:, DN_WIDTH:DN_WIDTH + d], rest[:, DN_WIDTH + d:]
    cw = jnp.pad(conv_w, ((0, SUBLANES - SHORT_CONV), (0, 0)))
    alog_row, dtb_row = _pad_lanes(a_log), _pad_lanes(dt_bias)
    zero_state = jnp.zeros((bsz, 2 * N_HEAD_GROUPS, DN_HEAD_DIM, HEAD_GROUP * DN_HEAD_DIM), F32)

    qkv_c, gates_c = _in_proj(ctx, shift_c, scale_c, norm_gain, cw, alog_row, dtb_row,
                              w_qkv, w_ab, [], False)
    _, _, s_ctx = _delta_rule(qkv_c, gates_c, zero_state)

    qkv, gates, z_dn, u_pool, z_pool = _in_proj(
        x, shift_x, scale_x, norm_gain, cw, alog_row, dtb_row, w_qkv, w_ab,
        [(w_zdn, BF16), (w_up, F32), (w_zp, BF16)], True)
    o_f, o_b, _ = _delta_rule(qkv, gates, s_ctx)
    pw = _bf(pool_w)
    ps = pool_scale.reshape(1, -1)
    y_pools = [_pool_group(u_pool, z_pool, pw, ps, g) for g in range(POOL_GROUPS)]
    return _out_proj(o_f, o_b, z_dn, y_pools, x, gate_x, dn_norm_gain, final_gain, _bf(w_out))


def kernel(x, c, ctx, c_ctx, w_mod, b_mod, norm_gain, w_in, conv_w, a_log, dt_bias, dn_norm_gain,
           pool_w, pool_scale, w_out, final_gain):
    bsz, length, d = x.shape
    depth = w_mod.shape[0]
    assert depth == 1, "single trunk layer"
    rows = -(-(bsz + 1) // SUBLANES) * SUBLANES
    cc = jnp.zeros((rows, d), F32).at[:bsz].set(c).at[bsz].set(c_ctx)
    mod = _mod_rows(cc, w_mod[0], b_mod[0])
    shift_x, scale_x, gate_x = (mod[:bsz, k * d:(k + 1) * d].reshape(bsz, 1, d) for k in range(3))
    shift_c, scale_c = (mod[bsz:bsz + 1, k * d:(k + 1) * d].reshape(1, 1, d) for k in range(2))
    return _layer(x, ctx, shift_x, scale_x, gate_x, shift_c, scale_c, norm_gain[0], w_in[0],
                  conv_w[0], a_log[0], dt_bias[0], dn_norm_gain[0], pool_w[0], pool_scale[0],
                  w_out[0], final_gain)
```

```python
import functools

import jax
import jax.numpy as jnp
from jax import lax
from jax.experimental import pallas as pl
from jax.experimental.pallas import tpu as pltpu

GRID_W = 64
DN_HEADS = 8
DN_HEAD_DIM = 128
DN_WIDTH = DN_HEADS * DN_HEAD_DIM
SHORT_CONV = 5
CHUNK = 64
POOL_GROUPS = 4
POOL_WINDOWS = (2, 4, 8, 16)
EPS = 1e-6

LANES = 128
SUBLANES = 8
HEAD_GROUP = 4
N_HEAD_GROUPS = DN_HEADS // HEAD_GROUP
CHUNKS_PER_STEP = 4
TOKEN_TILE = 256
COL_TILE = 512
VMEM_LIMIT = 56 * 1024 * 1024

F32 = jnp.float32
BF16 = jnp.bfloat16
HI = lax.Precision.HIGHEST


def _sigmoid(x):
    return 1.0 / (1.0 + jnp.exp(-x))


def _silu(x):
    return x * _sigmoid(x)


def _softplus(x):
    return jnp.maximum(x, 0.0) + jnp.log1p(jnp.exp(-jnp.abs(x)))


def _bf(x):
    return x.astype(BF16)


def _params(*sem):
    return pltpu.CompilerParams(dimension_semantics=sem, vmem_limit_bytes=VMEM_LIMIT)


def _mod_kernel(c_ref, w_ref, b_ref, o_ref):
    o_ref[...] = jnp.dot(_silu(c_ref[...]), w_ref[...], precision=HI,
                         preferred_element_type=F32) + b_ref[...]


def _mod_rows(cc, w_mod, b_mod):
    rows, d = cc.shape
    n = w_mod.shape[1]
    tn = 1024
    return pl.pallas_call(
        _mod_kernel,
        grid=(n // tn,),
        in_specs=[pl.BlockSpec((rows, d), lambda j: (0, 0)),
                  pl.BlockSpec((d, tn), lambda j: (0, j)),
                  pl.BlockSpec((1, tn), lambda j: (0, j))],
        out_specs=pl.BlockSpec((rows, tn), lambda j: (0, j)),
        out_shape=jax.ShapeDtypeStruct((rows, n), F32),
        compiler_params=_params("arbitrary"),
        name="mod_rows",
    )(cc, w_mod, b_mod.reshape(1, n))


def _in_proj_kernel(x_ref, xp_ref, xn_ref, shift_ref, scale_ref, gain_ref, cw_ref, alog_ref,
                    dtb_ref, wqkv_ref, wab_ref, *refs, n_extra, tm):
    w_refs = refs[:n_extra]
    qkv_ref, gates_ref = refs[n_extra:n_extra + 2]
    o_refs = refs[n_extra + 2:2 * n_extra + 2]
    hbe_ref, hb_ref = refs[2 * n_extra + 2:2 * n_extra + 4]
    ext_refs = refs[2 * n_extra + 4:]
    j = pl.program_id(1)
    nj = pl.num_programs(1)
    halo = SUBLANES
    pad = SHORT_CONV // 2
    tn = COL_TILE

    x = jnp.concatenate([xp_ref[0], x_ref[0], xn_ref[0]], axis=0)
    d = x.shape[-1]
    ms = jnp.sum(x * x, axis=-1, keepdims=True) * (1.0 / d)
    hn = (x * lax.rsqrt(ms + EPS)) * gain_ref[...]
    hn = hn * (1.0 + scale_ref[0]) + shift_ref[0]
    hbe_ref[...] = _bf(hn)
    hb_ref[...] = _bf(hn[halo:halo + tm])

    c = wqkv_ref.shape[1]
    not_first = (j > 0).astype(F32)
    not_last = (j < nj - 1).astype(F32)
    extra_chunks = [(w_ref, o_ref, n0) for w_ref, o_ref in zip(w_refs, o_refs)
                    for n0 in range(0, w_ref.shape[1], tn)]
    q_scale = float(DN_HEAD_DIM) ** -0.5
    rb = 128
    n_chunks = c // tn
    for ci in range(n_chunks):
        n0 = ci * tn
        p = jnp.dot(hbe_ref[...], wqkv_ref[:, n0:n0 + tn], preferred_element_type=F32)
        ext_ref = ext_refs[ci]
        ext_ref[0:halo, :] = p[0:halo] * not_first
        ext_ref[halo:halo + tm, :] = p[halo:halo + tm]
        ext_ref[halo + tm:, :] = p[halo + tm:] * not_last
        for w_ref, o_ref, e0 in extra_chunks[ci::n_chunks]:
            o_ref[0, :, e0:e0 + tn] = jnp.dot(hb_ref[...], w_ref[:, e0:e0 + tn],
                                              preferred_element_type=F32).astype(o_ref.dtype)
        for cb in range(n0 // LANES, (n0 + tn) // LANES):
            cs = slice(cb * LANES, (cb + 1) * LANES)
            es = slice(cb * LANES - n0, (cb + 1) * LANES - n0)
            taps = [cw_ref[t:t + 1, cs] for t in range(SHORT_CONV)]
            for r0 in range(0, tm, rb):
                nrow = rb + 2 * halo
                xt = ext_ref[r0:r0 + nrow, es]
                acc = None
                for t in range(SHORT_CONV):
                    sh = (pad - t) % nrow
                    rolled = xt if sh == 0 else pltpu.roll(xt, sh, 0)
                    term = rolled[halo:halo + rb] * taps[t]
                    acc = term if acc is None else acc + term
                y = _silu(acc)
                if cb < 2 * DN_HEADS:
                    inv = lax.rsqrt(jnp.sum(y * y, axis=-1, keepdims=True) + EPS)
                    y = y * (inv * q_scale if cb < DN_HEADS else inv)
                qkv_ref[0, r0:r0 + rb, cs] = y

    ab_all = jnp.dot(hb_ref[...], wab_ref[...], preferred_element_type=F32)
    row = lax.broadcasted_iota(jnp.int32, (CHUNK, CHUNK), 0)
    col = lax.broadcasted_iota(jnp.int32, (CHUNK, CHUNK), 1)
    lower = (row >= col).astype(F32)
    upper = (row <= col).astype(F32)
    lane = lax.broadcasted_iota(jnp.int32, (CHUNK, LANES), 1)
    for r0 in range(0, tm, CHUNK):
        ab = ab_all[r0:r0 + CHUNK]
        g = -jnp.exp(alog_ref[...]) * _softplus(ab + dtb_ref[...])
        cum_f = jnp.dot(lower, g, precision=HI, preferred_element_type=F32)
        cum_b = jnp.dot(upper, g, precision=HI, preferred_element_type=F32)
        gates_ref[0, r0:r0 + CHUNK, :] = jnp.where(
            lane < DN_HEADS, cum_f, jnp.where(lane < 2 * DN_HEADS, cum_b, _sigmoid(ab)))


def _in_proj(h, shift, scale, gain, conv_w, alog_row, dtb_row, w_qkv, w_ab, extra, per_batch):
    bsz, length, d = h.shape
    tm = min(TOKEN_TILE, length)
    halo = SUBLANES
    per = tm // halo
    nblk8 = length // halo
    c = w_qkv.shape[1]
    mod_map = (lambda b, j: (b, 0, 0)) if per_batch else (lambda b, j: (0, 0, 0))
    const = lambda shape: pl.BlockSpec(shape, lambda b, j: tuple(0 for _ in shape))
    in_specs = [pl.BlockSpec((1, tm, d), lambda b, j: (b, j, 0)),
                pl.BlockSpec((1, halo, d), lambda b, j: (b, jnp.maximum(j * per - 1, 0), 0)),
                pl.BlockSpec((1, halo, d), lambda b, j: (b, jnp.minimum((j + 1) * per, nblk8 - 1), 0)),
                pl.BlockSpec((1, 1, d), mod_map),
                pl.BlockSpec((1, 1, d), mod_map),
                const((1, d)), const(conv_w.shape), const((1, LANES)), const((1, LANES)),
                const(w_qkv.shape), const(w_ab.shape)]
    in_specs += [const(w.shape) for w, _ in extra]
    tok = lambda w: pl.BlockSpec((1, tm, w), lambda b, j: (b, j, 0))
    out_specs = [tok(c), tok(LANES)] + [tok(w.shape[1]) for w, _ in extra]
    out_shape = [jax.ShapeDtypeStruct((bsz, length, c), F32),
                 jax.ShapeDtypeStruct((bsz, length, LANES), F32)]
    out_shape += [jax.ShapeDtypeStruct((bsz, length, w.shape[1]), dt) for w, dt in extra]
    return pl.pallas_call(
        functools.partial(_in_proj_kernel, n_extra=len(extra), tm=tm),
        grid=(bsz, length // tm),
        in_specs=in_specs,
        out_specs=out_specs,
        out_shape=out_shape,
        scratch_shapes=[pltpu.VMEM((tm + 2 * halo, d), BF16), pltpu.VMEM((tm, d), BF16)]
        + [pltpu.VMEM((tm + 2 * halo, COL_TILE), F32)] * (c // COL_TILE),
        compiler_params=_params("parallel", "arbitrary"),
        name="in_proj",
    )(h, h, h, shift, scale, gain.reshape(1, d), conv_w, alog_row, dtb_row, w_qkv, w_ab,
      *[w for w, _ in extra])


def _cat_index():
    shape = (CHUNK, HEAD_GROUP * CHUNK)
    row = lax.broadcasted_iota(jnp.int32, shape, 0)
    lane = lax.broadcasted_iota(jnp.int32, shape, 1)
    return row, jnp.bitwise_and(lane, CHUNK - 1), jnp.right_shift(lane, CHUNK.bit_length() - 1)


def _block_diag_dot(l_cat, r_cat, bd_mask):
    w = jnp.where(bd_mask, jnp.concatenate([r_cat] * HEAD_GROUP, axis=0), 0.0)
    return jnp.dot(_bf(l_cat), _bf(w), preferred_element_type=F32)


def _delta_prep(q_ref, k_ref, v_ref, g_ref, r0, hg, d, idx):
    c, dk = CHUNK, DN_HEAD_DIM
    lower = d == 0
    row, col, mat = idx
    incl = (row >= col) if lower else (row <= col)
    last = c - 1 if lower else 0
    gates = g_ref[0, r0:r0 + c, :]
    ks, kbs, qs, rhs, q_decs, k_tails, s_decs = [], [], [], [], [], [], []
    cum_cat = None
    for u in range(HEAD_GROUP):
        head = hg * HEAD_GROUP + u
        cs = slice(head * dk, (head + 1) * dk)
        ci = d * DN_HEADS + head
        bi = 2 * DN_HEADS + ci
        cum_col = gates[:, ci:ci + 1]
        cum = jnp.broadcast_to(cum_col, (c, dk))
        beta = jnp.broadcast_to(gates[:, bi:bi + 1], (c, dk))
        q, k, v = q_ref[0, r0:r0 + c, cs], k_ref[0, r0:r0 + c, cs], v_ref[0, r0:r0 + c, cs]
        kb = k * beta
        e_cum = jnp.exp(cum)
        cum_last = cum[last:last + 1, :]
        ks.append(k), kbs.append(kb), qs.append(q)
        rhs.append(jnp.concatenate([v * beta, kb * e_cum], axis=1))
        q_decs.append(q * e_cum)
        k_tails.append(k * jnp.exp(cum_last - cum))
        s_decs.append(jnp.exp(cum_last))
        cc = jnp.broadcast_to(cum_col, (c, HEAD_GROUP * c))
        cum_cat = cc if u == 0 else jnp.where(mat == u, cc, cum_cat)
    cum_row = jnp.sum(jnp.where(row == col, cum_cat, 0.0), axis=0, keepdims=True)
    decay = jnp.exp(jnp.where(incl, cum_cat - cum_row, -1e30))
    return dict(k=ks, kb=kbs, q=qs, rhs=rhs, q_dec=q_decs,
                k_tail_t=jnp.concatenate(k_tails, axis=0).T,
                s_dec=jnp.concatenate(s_decs, axis=1), decay=decay, lower=lower)


def _delta_gram(grp, idx):
    c = CHUNK
    row, col, mat = idx
    lhs = jnp.concatenate([x for kb, q in zip(grp["kb"], grp["q"]) for x in (kb, q)], axis=0)
    kcat = jnp.concatenate(grp["k"], axis=0)
    m = lax.dot_general(_bf(lhs), _bf(kcat), (((1,), (1,)), ((), ())), preferred_element_type=F32)
    a_cat = qk_cat = None
    for u in range(HEAD_GROUP):
        a_u, qk_u = m[2 * u * c:(2 * u + 1) * c], m[(2 * u + 1) * c:(2 * u + 2) * c]
        a_cat = a_u if u == 0 else jnp.where(mat == u, a_u, a_cat)
        qk_cat = qk_u if u == 0 else jnp.where(mat == u, qk_u, qk_cat)
    grp["a"] = jnp.where(row == col, 0.0, a_cat * grp["decay"])
    grp["qk"] = qk_cat * grp["decay"]


def _off_mask(idx, lower, s):
    row, col, _ = idx
    sh = s.bit_length() - 1
    rb, cb = jnp.right_shift(row, sh), jnp.right_shift(col, sh)
    if lower:
        return (jnp.bitwise_and(rb, 1) == 1) & (cb == rb - 1)
    return (jnp.bitwise_and(rb, 1) == 0) & (cb == rb + 1)


def _recur_ws(grp, sol, state):
    c, dk = CHUNK, DN_HEAD_DIM
    zero = jnp.zeros((dk, dk), BF16)
    sb = _bf(state)
    ws = []
    for u in range(0, HEAD_GROUP, 2):
        lhs = jnp.concatenate(
            [jnp.concatenate([sol[t][:, dk:], grp["q_dec"][t]], axis=0) for t in (u, u + 1)], axis=1)
        w = jnp.concatenate(
            [jnp.concatenate([sb[:, u * dk:(u + 1) * dk], zero], axis=1),
             jnp.concatenate([zero, sb[:, (u + 1) * dk:(u + 2) * dk]], axis=1)], axis=0)
        ws.append(jnp.dot(_bf(lhs), w, preferred_element_type=F32))
    ws = jnp.concatenate(ws, axis=1)
    v_new = jnp.concatenate([s[:, 0:dk] for s in sol], axis=1) - ws[0:c]
    return ws, v_new


def _recur_out(grp, ws, v_new, state, lane_head):
    c = CHUNK
    vb = _bf(v_new)
    bd_v = jnp.concatenate([jnp.where(lane_head == u, vb, jnp.zeros_like(vb))
                            for u in range(HEAD_GROUP)], axis=0)
    res = jnp.dot(_bf(jnp.concatenate([grp["qk"], grp["k_tail_t"]], axis=0)), bd_v,
                  preferred_element_type=F32)
    return ws[c:2 * c] + res[0:c], state * grp["s_dec"] + res[c:]


def _delta_kernel(qf_ref, kf_ref, vf_ref, gf_ref, qb_ref, kb_ref, vb_ref, gb_ref,
                  s0_ref, of_ref, ob_ref, s_ref, state_ref):
    i = pl.program_id(1)
    ni = pl.num_programs(1)
    c = CHUNK
    n_state = 2 * N_HEAD_GROUPS

    @pl.when(i == 0)
    def _():
        state_ref[...] = s0_ref[0]

    idx = _cat_index()
    n = HEAD_GROUP * c
    sh = c.bit_length() - 1
    bd_mask = (jnp.right_shift(lax.broadcasted_iota(jnp.int32, (n, n), 0), sh)
               == jnp.right_shift(lax.broadcasted_iota(jnp.int32, (n, n), 1), sh))
    lane_head = lax.broadcasted_iota(jnp.int32, (c, HEAD_GROUP * DN_HEAD_DIM), 1) // DN_HEAD_DIM
    eye = (idx[0] == idx[1]).astype(F32)

    keys, groups = [], []
    for slot in range(CHUNKS_PER_STEP):
        for d in range(2):
            for hg in range(N_HEAD_GROUPS):
                refs = (qf_ref, kf_ref, vf_ref, gf_ref) if d == 0 else (qb_ref, kb_ref, vb_ref, gb_ref)
                r0 = slot * c if d == 0 else (CHUNKS_PER_STEP - 1 - slot) * c
                keys.append((slot, d, hg, r0))
                groups.append(_delta_prep(*refs, r0, hg, d, idx))
    for grp in groups:
        _delta_gram(grp, idx)

    xs = [eye - jnp.where(_off_mask(idx, g["lower"], 1), g["a"], 0.0) for g in groups]
    s = 2
    while s < c:
        ls = [jnp.where(_off_mask(idx, g["lower"], s), g["a"], 0.0) for g in groups]
        ps = [_block_diag_dot(l, x, bd_mask) for l, x in zip(ls, xs)]
        xs = [x - _block_diag_dot(x, p, bd_mask) for x, p in zip(xs, ps)]
        s *= 2

    sols = [[jnp.dot(_bf(x[:, u * c:(u + 1) * c]), _bf(g["rhs"][u]), preferred_element_type=F32)
             for u in range(HEAD_GROUP)] for g, x in zip(groups, xs)]

    o_refs = (of_ref, ob_ref)
    states = [state_ref[t] for t in range(n_state)]
    hgw = HEAD_GROUP * DN_HEAD_DIM
    per_slot = 2 * N_HEAD_GROUPS
    for slot in range(CHUNKS_PER_STEP):
        sel = range(slot * per_slot, (slot + 1) * per_slot)
        tids = [keys[n][1] * N_HEAD_GROUPS + keys[n][2] for n in sel]
        wv = [_recur_ws(groups[n], sols[n], states[t]) for n, t in zip(sel, tids)]
        for n, t, (ws, v_new) in zip(sel, tids, wv):
            _, d, hg, r0 = keys[n]
            out, states[t] = _recur_out(groups[n], ws, v_new, states[t], lane_head)
            o_refs[d][0, r0:r0 + c, hg * hgw:(hg + 1) * hgw] = out.astype(o_refs[d].dtype)
    for t in range(n_state):
        state_ref[t] = states[t]

    @pl.when(i == ni - 1)
    def _():
        s_ref[0] = state_ref[...]


def _delta_rule(qkv, gates, s0):
    bsz, length, _ = qkv.shape
    rows = CHUNKS_PER_STEP * CHUNK
    nb = length // rows
    assert length % rows == 0, "sequence length must be a multiple of the delta-rule step"
    blk = (1, rows, DN_WIDTH)
    fwd = lambda col: pl.BlockSpec(blk, lambda b, i: (b, i, col))
    bwd = lambda col: pl.BlockSpec(blk, lambda b, i: (b, nb - 1 - i, col))
    gblk = (1, rows, LANES)
    sspec = pl.BlockSpec((1,) + s0.shape[1:], lambda b, i: (b, 0, 0, 0))
    o_shape = jax.ShapeDtypeStruct((bsz, length, DN_WIDTH), BF16)
    return pl.pallas_call(
        _delta_kernel,
        grid=(bsz, nb),
        in_specs=[fwd(0), fwd(1), fwd(2), pl.BlockSpec(gblk, lambda b, i: (b, i, 0)),
                  bwd(0), bwd(1), bwd(2), pl.BlockSpec(gblk, lambda b, i: (b, nb - 1 - i, 0)),
                  sspec],
        out_specs=[pl.BlockSpec(blk, lambda b, i: (b, i, 0)),
                   pl.BlockSpec(blk, lambda b, i: (b, nb - 1 - i, 0)),
                   sspec],
        out_shape=[o_shape, o_shape, jax.ShapeDtypeStruct(s0.shape, F32)],
        scratch_shapes=[pltpu.VMEM(s0.shape[1:], F32)],
        compiler_params=_params("parallel", "arbitrary"),
        name="delta_rule",
    )(qkv, qkv, qkv, gates, qkv, qkv, qkv, gates, s0)


def _pool_kernel(u_ref, z_ref, w_ref, scale_ref, y_ref, cs_ref, d_ref, *, win, rows):
    half = win // 2
    gw = GRID_W
    ch = u_ref.shape[-1]
    pad = SUBLANES
    n = gw + 2 * pad
    cidx = lax.broadcasted_iota(jnp.int32, (gw, ch), 0)
    col_cnt = (jnp.minimum(cidx - half + win, gw) - jnp.maximum(cidx - half, 0)).astype(F32)
    zeros = jnp.zeros((gw, ch), F32)
    zpad = jnp.zeros((pad, ch), F32)
    for r in list(range(half)) + list(range(half + rows, rows + win)):
        cs_ref[r * gw:(r + 1) * gw, :] = zeros

    def blk(r):
        return pl.ds(pl.multiple_of(r * gw, gw), gw)

    def col_pass(r, carry):
        xp = jnp.concatenate([zpad, u_ref[0, blk(r), :], zpad], axis=0)
        s = 1
        while s < win:
            xp = xp + pltpu.roll(xp, n - s, 0)
            s *= 2
        off = pad - half
        cs_ref[blk(r + half), :] = (xp if off == 0 else pltpu.roll(xp, n - off, 0))[0:gw]
        return carry

    lax.fori_loop(0, rows, col_pass, 0)

    s = 1
    while s < win:
        def row_stage(r, carry, s=s):
            cs_ref[blk(r), :] = cs_ref[blk(r), :] + cs_ref[blk(r + s), :]
            return carry

        lax.fori_loop(0, rows + win - 2 * s, row_stage, 0)
        s *= 2

    def finish(r, carry):
        row_cnt = (jnp.minimum(r - half + win, rows) - jnp.maximum(r - half, 0)).astype(F32)
        d_ref[blk(r), :] = cs_ref[blk(r), :] / (col_cnt * row_cnt) - u_ref[0, blk(r), :]
        return carry

    lax.fori_loop(0, rows, finish, 0)

    tm = min(512, rows * gw)
    for m0 in range(0, rows * gw, tm):
        y = jnp.dot(_bf(d_ref[m0:m0 + tm, :]), w_ref[0], preferred_element_type=F32)
        y = y * scale_ref[...] * _silu(z_ref[0, m0:m0 + tm, :].astype(F32))
        y_ref[0, m0:m0 + tm, :] = y.astype(y_ref.dtype)


def _pool_group(u_pool, z_pool, pool_w_bf16, pool_scale_row, g):
    bsz, length, width = u_pool.shape
    ch = width // POOL_GROUPS
    rows = length // GRID_W
    win = POOL_WINDOWS[g]
    assert win // 2 <= SUBLANES and win & (win - 1) == 0, "window must be a power of two <= 16"
    return pl.pallas_call(
        functools.partial(_pool_kernel, win=win, rows=rows),
        grid=(bsz,),
        in_specs=[pl.BlockSpec((1, length, ch), lambda b: (b, 0, g)),
                  pl.BlockSpec((1, length, ch), lambda b: (b, 0, g)),
                  pl.BlockSpec((1, ch, ch), lambda b: (g, 0, 0)),
                  pl.BlockSpec((1, ch), lambda b: (0, g))],
        out_specs=pl.BlockSpec((1, length, ch), lambda b: (b, 0, 0)),
        out_shape=jax.ShapeDtypeStruct((bsz, length, ch), BF16),
        scratch_shapes=[pltpu.VMEM(((rows + win) * GRID_W, ch), F32),
                        pltpu.VMEM((length, ch), F32)],
        compiler_params=_params("parallel"),
        name=f"pool_w{win}",
    )(u_pool, z_pool, pool_w_bf16, pool_scale_row)


def _out_proj_kernel(of_ref, ob_ref, z_ref, y0_ref, y1_ref, y2_ref, y3_ref, x_ref, gate_ref,
                     dng_ref, fg_ref, w_ref, out_ref):
    o = of_ref[0].astype(F32) + ob_ref[0].astype(F32)
    z = z_ref[0].astype(F32)
    parts = []
    for h in range(DN_HEADS):
        cs = slice(h * DN_HEAD_DIM, (h + 1) * DN_HEAD_DIM)
        oh = o[:, cs]
        ms = jnp.sum(oh * oh, axis=-1, keepdims=True) * (1.0 / DN_HEAD_DIM)
        oh = (oh * lax.rsqrt(ms + EPS)) * dng_ref[...]
        parts.append(_bf(oh * _silu(z[:, cs])))
    acc = jnp.dot(jnp.concatenate(parts, axis=1), w_ref[0:DN_WIDTH, :], preferred_element_type=F32)
    ch = y0_ref.shape[-1]
    for g, y_ref in enumerate((y0_ref, y1_ref, y2_ref, y3_ref)):
        r0 = DN_WIDTH + g * ch
        acc = acc + jnp.dot(y_ref[0], w_ref[r0:r0 + ch, :], preferred_element_type=F32)
    xn = x_ref[0] + gate_ref[0] * acc
    d = xn.shape[-1]
    ms = jnp.sum(xn * xn, axis=-1, keepdims=True) * (1.0 / d)
    out_ref[0] = (xn * lax.rsqrt(ms + EPS)) * fg_ref[...]


def _out_proj(o_f, o_b, z_dn, y_pools, x, gate, dn_gain, final_gain, w_out_bf16):
    bsz, length, d = x.shape
    tm = min(TOKEN_TILE, length)
    ch = y_pools[0].shape[-1]
    tok = lambda w: pl.BlockSpec((1, tm, w), lambda b, j: (b, j, 0))
    const = lambda shape: pl.BlockSpec(shape, lambda b, j: tuple(0 for _ in shape))
    return pl.pallas_call(
        _out_proj_kernel,
        grid=(bsz, length // tm),
        in_specs=[tok(DN_WIDTH), tok(DN_WIDTH), tok(DN_WIDTH), tok(ch), tok(ch), tok(ch), tok(ch),
                  tok(d), pl.BlockSpec((1, 1, d), lambda b, j: (b, 0, 0)),
                  const((1, DN_HEAD_DIM)), const((1, d)), const(w_out_bf16.shape)],
        out_specs=tok(d),
        out_shape=jax.ShapeDtypeStruct((bsz, length, d), F32),
        compiler_params=_params("parallel", "arbitrary"),
        name="out_proj",
    )(o_f, o_b, z_dn, *y_pools, x, gate, dn_gain.reshape(1, DN_HEAD_DIM),
      final_gain.reshape(1, d), w_out_bf16)


def _pad_lanes(row, n=LANES):
    row = row.reshape(1, -1).astype(F32)
    return jnp.pad(row, ((0, 0), (0, n - row.shape[1])))


def _layer(x, ctx, shift_x, scale_x, gate_x, shift_c, scale_c, norm_gain, w_in, conv_w, a_log,
           dt_bias, dn_norm_gain, pool_w, pool_scale, w_out, final_gain):
    bsz, length, d = x.shape
    w3 = 3 * DN_WIDTH
    nab = 4 * DN_HEADS
    w_qkv = _bf(w_in[:, :w3])
    w_ab = _bf(jnp.pad(w_in[:, w3:w3 + nab], ((0, 0), (0, LANES - nab))))
    rest = _bf(w_in[:, w3 + nab:])
    w_zdn, w_up, w_zp = rest[:, :DN_WIDTH], rest[:, DN_WIDTH:DN_WIDTH + d], rest[:, DN_WIDTH + d:]
    cw = jnp.pad(conv_w, ((0, SUBLANES - SHORT_CONV), (0, 0)))
    alog_row, dtb_row = _pad_lanes(a_log), _pad_lanes(dt_bias)
    zero_state = jnp.zeros((bsz, 2 * N_HEAD_GROUPS, DN_HEAD_DIM, HEAD_GROUP * DN_HEAD_DIM), F32)

    qkv_c, gates_c = _in_proj(ctx, shift_c, scale_c, norm_gain, cw, alog_row, dtb_row,
                              w_qkv, w_ab, [], False)
    _, _, s_ctx = _delta_rule(qkv_c, gates_c, zero_state)

    qkv, gates, z_dn, u_pool, z_pool = _in_proj(
        x, shift_x, scale_x, norm_gain, cw, alog_row, dtb_row, w_qkv, w_ab,
        [(w_zdn, BF16), (w_up, F32), (w_zp, BF16)], True)
    o_f, o_b, _ = _delta_rule(qkv, gates, s_ctx)
    pw = _bf(pool_w)
    ps = pool_scale.reshape(1, -1)
    y_pools = [_pool_group(u_pool, z_pool, pw, ps, g) for g in range(POOL_GROUPS)]
    return _out_proj(o_f, o_b, z_dn, y_pools, x, gate_x, dn_norm_gain, final_gain, _bf(w_out))


def kernel(x, c, ctx, c_ctx, w_mod, b_mod, norm_gain, w_in, conv_w, a_log, dt_bias, dn_norm_gain,
           pool_w, pool_scale, w_out, final_gain):
    bsz, length, d = x.shape
    depth = w_mod.shape[0]
    assert depth == 1, "single trunk layer"
    rows = -(-(bsz + 1) // SUBLANES) * SUBLANES
    cc = jnp.zeros((rows, d), F32).at[:bsz].set(c).at[bsz].set(c_ctx)
    mod = _mod_rows(cc, w_mod[0], b_mod[0])
    shift_x, scale_x, gate_x = (mod[:bsz, k * d:(k + 1) * d].reshape(bsz, 1, d) for k in range(3))
    shift_c, scale_c = (mod[bsz:bsz + 1, k * d:(k + 1) * d].reshape(1, 1, d) for k in range(2))
    return _layer(x, ctx, shift_x, scale_x, gate_x, shift_c, scale_c, norm_gain[0], w_in[0],
                  conv_w[0], a_log[0], dt_bias[0], dn_norm_gain[0], pool_w[0], pool_scale[0],
                  w_out[0], final_gain)
```
